```python
import math
import jax
import jax.numpy as jnp
from jax import lax
import numpy as np

D_MODEL = 2048
BATCH = 2
SEQ = 16384
DEPTH = 2

D_MIX = D_MODEL
RET_W = 3 * D_MIX // 8
RET_HD = 128
RET_HEADS = RET_W // RET_HD
RET_CHUNK = 128
ROPE_BASE = 10000.0
RWKV_W = 3 * D_MIX // 8
RWKV_HD = 64
RWKV_HEADS = RWKV_W // RWKV_HD
DECAY_LORA = 64
AAA_LORA = 64
GATE_LORA = 128
S5_W = D_MIX - RET_W - RWKV_W
S5_GROUP = 16
S5_GROUPS = S5_W // S5_GROUP
S5_STATE = 64
RET_COLS = 4 * RET_W
RWKV_COLS = 3 * RWKV_W + 2 * DECAY_LORA + 2 * AAA_LORA + GATE_LORA
IN_COLS = RET_COLS + RWKV_COLS + S5_W
N_GROUPS = 4
EXPERTS_PER_GROUP = 4
N_EXPERTS = N_GROUPS * EXPERTS_PER_GROUP
TOP_EXPERTS = 2
EXPERT_FF = D_MODEL // 4
PLE_DIM = 256
LN_EPS = 1e-5
RWKV_LNX_EPS = 64e-5

kernel_name = 'hymba_style_hybrid_encoder'


def _layernorm(x, w, b, eps=LN_EPS):
    xf = x.astype(jnp.float32)
    mu = jnp.mean(xf, axis=-1, keepdims=True)
    var = jnp.mean(jnp.square(xf - mu), axis=-1, keepdims=True)
    y = (xf - mu) * lax.rsqrt(var + eps) * w.astype(jnp.float32) + b.astype(jnp.float32)
    return y.astype(x.dtype)


def _rope(t, positions):
    half = t.shape[-1] // 2
    inv = ROPE_BASE ** (-jnp.arange(half, dtype=jnp.float32) / half)
    ang = positions.astype(jnp.float32)[..., None] * inv
    cos = jnp.cos(ang)[:, :, None, :]
    sin = jnp.sin(ang)[:, :, None, :]
    t1, t2 = t[..., :half], t[..., half:]
    return jnp.concatenate([t1 * cos - t2 * sin, t1 * sin + t2 * cos], axis=-1)


def _retention_mixer(z, positions):
    B_, S_, _ = z.shape
    H, Dh, C = RET_HEADS, RET_HD, RET_CHUNK
    nC = S_ // C
    q, k, v, g = jnp.split(z.astype(jnp.float32), 4, axis=-1)
    q = _rope(q.reshape(B_, S_, H, Dh), positions) * (Dh ** -0.5)
    k = _rope(k.reshape(B_, S_, H, Dh), positions)
    v = v.reshape(B_, S_, H, Dh)
    qc, kc, vc = [t.reshape(B_, nC, C, H, Dh) for t in (q, k, v)]
    lg = jnp.log(1.0 - 2.0 ** (-5.0 - jnp.arange(H, dtype=jnp.float32)))
    pos = jnp.arange(C, dtype=jnp.float32)
    d_intra = jnp.exp(lg[:, None, None] * jnp.abs(pos[:, None] - pos[None, :]))
    scores = jnp.einsum('bnihd,bnjhd->bnhij', qc, kc) * d_intra
    y = jnp.einsum('bnhij,bnjhe->bnihe', scores, vc)
    kv_f = jnp.einsum('bnjhd,hj,bnjhe->nbhde', kc, jnp.exp(lg[:, None] * (C - 1 - pos)), vc)
    kv_b = jnp.einsum('bnjhd,hj,bnjhe->nbhde', kc, jnp.exp(lg[:, None] * pos), vc)
    decay_c = jnp.exp(lg * C)[None, :, None, None]

    def carry_step(state, kv):
        return decay_c * state + kv, state

    s0 = jnp.zeros((B_, H, Dh, Dh), jnp.float32)
    _, s_f = lax.scan(carry_step, s0, kv_f)
    _, s_b = lax.scan(carry_step, s0, kv_b, reverse=True)
    y = y + jnp.einsum('bnihd,hi,nbhde->bnihe', qc, jnp.exp(lg[:, None] * (pos + 1.0)), s_f)
    y = y + jnp.einsum('bnihd,hi,nbhde->bnihe', qc, jnp.exp(lg[:, None] * (C - pos)), s_b)
    y = y.reshape(B_, S_, H, Dh)
    y = y * lax.rsqrt(jnp.mean(jnp.square(y), axis=-1, keepdims=True) + 1e-6)
    return y.reshape(B_, S_, RET_W) * jax.nn.silu(g)


def _rwkv7_mixer(z, mu_prev, mu_next, w0, w_up, a0, a_up, g_up, k_k, k_a, r_k, lnx_w, lnx_b):
    B_, S_, _ = z.shape
    H, N = RWKV_HEADS, RWKV_HD
    zf = z.astype(jnp.float32)
    z_prev = jnp.pad(zf[:, :-1], ((0, 0), (1, 0), (0, 0)))
    z_next = jnp.pad(zf[:, 1:], ((0, 0), (0, 1), (0, 0)))
    zf = zf + mu_prev * (z_prev - zf) + mu_next * (z_next - zf)
    idx = [RWKV_W, 2 * RWKV_W, 3 * RWKV_W, 3 * RWKV_W + 2 * DECAY_LORA,
           3 * RWKV_W + 2 * DECAY_LORA + 2 * AAA_LORA]
    r, k, v, wd, ad, gd = jnp.split(zf, idx, axis=-1)
    w_raw = w0 + jnp.einsum('bsdr,drc->bsdc', jnp.tanh(wd.reshape(B_, S_, 2, DECAY_LORA)), w_up)
    decay = jnp.exp(-jnp.exp(-jax.nn.softplus(-w_raw) - 0.5))
    a = jax.nn.sigmoid(a0 + jnp.einsum('bsdr,drc->bsdc', ad.reshape(B_, S_, 2, AAA_LORA), a_up))
    g = jnp.einsum('bsr,rc->bsc', jax.nn.sigmoid(gd), g_up)
    kk = (k * k_k).reshape(B_, S_, H, N)
    kk = (kk / jnp.maximum(jnp.sqrt(jnp.sum(jnp.square(kk), axis=-1, keepdims=True)), 1e-12)).reshape(B_, S_, RWKV_W)
    k_dir = k[:, :, None, :] * (1.0 + (a - 1.0) * k_a)
    b_dir = kk[:, :, None, :] * a

    def both(t):
        return jnp.stack([t, t], axis=2)

    def time_major(t):
        t = jnp.stack([t[:, :, 0], jnp.flip(t[:, :, 1], axis=1)], axis=0)
        return t.reshape(2, B_, S_, H, N).transpose(2, 0, 1, 3, 4)

    xs = (time_major(both(r)), time_major(decay), time_major(k_dir),
          time_major(both(v)), time_major(both(kk)), time_major(b_dir))

    def step(state, inp):
        r_t, w_t, k_t, v_t, kk_t, b_t = inp
        sa = jnp.einsum('dbhvk,dbhk->dbhv', state, -kk_t)
        state = (state * w_t[..., None, :] + sa[..., :, None] * b_t[..., None, :]
                 + v_t[..., :, None] * k_t[..., None, :])
        return state, jnp.einsum('dbhvk,dbhk->dbhv', state, r_t)

    state0 = jnp.zeros((2, B_, H, N, N), jnp.float32)
    _, y = lax.scan(step, state0, xs)
    y = (y[:, 0] + jnp.flip(y[:, 1], axis=0)).transpose(1, 0, 2, 3)
    mu = jnp.mean(y, axis=-1, keepdims=True)
    var = jnp.mean(jnp.square(y - mu), axis=-1, keepdims=True)
    y = ((y - mu) * lax.rsqrt(var + RWKV_LNX_EPS)).reshape(B_, S_, RWKV_W) * lnx_w + lnx_b
    rh, kh, vh = [t.reshape(B_, S_, H, N) for t in (r, k, v)]
    bonus = jnp.sum(rh * kh * r_k, axis=-1, keepdims=True) * vh
    return (y + bonus.reshape(B_, S_, RWKV_W)) * g


def _cplx_affine_combine(e1, e2):
    a1r, a1i, b1r, b1i = e1
    a2r, a2i, b2r, b2i = e2
    return (a2r * a1r - a2i * a1i, a2r * a1i + a2i * a1r,
            a2r * b1r - a2i * b1i + b2r, a2r * b1i + a2i * b1r + b2i)


def _s5_mixer(u, lam_re, lam_im, log_dt, b_re, b_im, c_re, c_im, d_skip, glu_w, glu_b):
    B_, S_, _ = u.shape
    f32 = jnp.float32
    uf = u.astype(f32)
    ug = uf.reshape(B_, S_, S5_GROUPS, S5_GROUP)
    y = uf * d_skip.astype(f32)
    for d in range(2):
        lr, li = lam_re[d].astype(f32), lam_im[d].astype(f32)
        dt = jnp.exp(log_dt[d].astype(f32))[:, None]
        mag = jnp.exp(lr * dt)
        ab_re, ab_im = mag * jnp.cos(li * dt), mag * jnp.sin(li * dt)
        den = lr * lr + li * li
        nr, ni = ab_re - 1.0, ab_im
        cr = (nr * lr + ni * li) / den
        ci = (ni * lr - nr * li) / den
        br, bi = b_re[d].astype(f32), b_im[d].astype(f32)
        bb_re = cr[..., None] * br - ci[..., None] * bi
        bb_im = cr[..., None] * bi + ci[..., None] * br
        bu_re = jnp.einsum('bsgh,gph->bsgp', ug, bb_re)
        bu_im = jnp.einsum('bsgh,gph->bsgp', ug, bb_im)
        a_re = jnp.broadcast_to(ab_re, (1, S_) + ab_re.shape)
        a_im = jnp.broadcast_to(ab_im, (1, S_) + ab_im.shape)
        _, _, x_re, x_im = lax.associative_scan(
            _cplx_affine_combine, (a_re, a_im, bu_re, bu_im), reverse=(d == 1), axis=1)
        y_d = (jnp.einsum('ghp,bsgp->bsgh', c_re[d].astype(f32), x_re)
               - jnp.einsum('ghp,bsgp->bsgh', c_im[d].astype(f32), x_im))
        y = y + y_d.reshape(B_, S_, S5_W)
    zg = jax.nn.gelu(y)
    return zg * jax.nn.sigmoid(zg @ glu_w.astype(f32) + glu_b.astype(f32))


def _hier_moe(x, router_g, router_g_b, router_e, router_e_b, w1, w3, w2):
    B_, S_, D = x.shape
    xt = x.reshape(-1, D)
    g_logits = (xt @ router_g).astype(jnp.float32) + router_g_b.astype(jnp.float32)
    g_prob = jax.nn.softmax(g_logits, axis=-1)
    p_top, grp = lax.top_k(g_prob, 1)
    e_all = jnp.einsum('td,gde->tge', xt, router_e).astype(jnp.float32) + router_e_b.astype(jnp.float32)
    e_logits = jnp.take_along_axis(e_all, grp[:, :, None], axis=1)[:, 0]
    v_top, i_top = lax.top_k(e_logits, TOP_EXPERTS)
    w_top = jax.nn.softmax(v_top, axis=-1) * p_top
    eid = grp * EXPERTS_PER_GROUP + i_top
    gate = jnp.sum(jax.nn.one_hot(eid, N_EXPERTS, dtype=jnp.float32) * w_top[..., None], axis=1)
    y = jnp.zeros(xt.shape, jnp.float32)
    for e in range(N_EXPERTS):
        h = jax.nn.silu(xt @ w1[e]) * (xt @ w3[e])
        y = y + gate[:, e:e + 1] * (h @ w2[e])
    return y.reshape(B_, S_, D).astype(x.dtype)


def setup_inputs(seed: int = 0) -> dict:
    key = jax.random.key(seed)
    ks = jax.random.split(key, 40)
    L = DEPTH
    beta = (8.0 * DEPTH) ** -0.25

    def nrm(i, shape, scale):
        return scale * jax.random.normal(ks[i], shape, jnp.float32)

    def unif(i, shape, lo, hi):
        return jax.random.uniform(ks[i], shape, dtype=jnp.float32, minval=lo, maxval=hi)

    positions = (jnp.arange(SEQ, dtype=jnp.int32)[None, :]
                 + jax.random.randint(ks[2], (BATCH, 1), 0, 1024, dtype=jnp.int32))
    s5_lam_im = jnp.pi * jnp.arange(S5_STATE, dtype=jnp.float32) + nrm(18, (L, 2, S5_GROUPS, S5_STATE), 0.01)
    return {
        'x': nrm(0, (BATCH, SEQ, D_MODEL), 1.0),
        'p': nrm(1, (L, BATCH, SEQ, PLE_DIM), 1.0),
        'positions': positions,
        'w_in': nrm(3, (L, D_MODEL, IN_COLS), D_MODEL ** -0.5),
        'w_out': nrm(4, (L, D_MIX, D_MODEL), beta * D_MIX ** -0.5),
        'rwkv_mu_prev': unif(5, (L, RWKV_COLS), 0.0, 0.5),
        'rwkv_mu_next': unif(6, (L, RWKV_COLS), 0.0, 0.5),
        'rwkv_w0': unif(7, (L, 2, RWKV_W), -6.0, -1.0),
        'rwkv_w_up': nrm(8, (L, 2, DECAY_LORA, RWKV_W), 0.1 * DECAY_LORA ** -0.5),
        'rwkv_a0': nrm(9, (L, 2, RWKV_W), 0.1),
        'rwkv_a_up': nrm(10, (L, 2, AAA_LORA, RWKV_W), 0.5 * AAA_LORA ** -0.5),
        'rwkv_g_up': nrm(11, (L, GATE_LORA, RWKV_W), GATE_LORA ** -0.5),
        'rwkv_k_k': 0.85 + nrm(12, (L, RWKV_W), 0.05),
        'rwkv_k_a': 1.0 + nrm(13, (L, RWKV_W), 0.05),
        'rwkv_r_k': nrm(14, (L, RWKV_HEADS, RWKV_HD), 0.1),
        'rwkv_lnx_w': 1.0 + nrm(15, (L, RWKV_W), 0.05),
        'rwkv_lnx_b': nrm(16, (L, RWKV_W), 0.01),
        's5_lam_re': -0.5 + nrm(17, (L, 2, S5_GROUPS, S5_STATE), 0.01),
        's5_lam_im': s5_lam_im,
        's5_log_dt': unif(19, (L, 2, S5_GROUPS), math.log(1e-3), math.log(1e-1)),
        's5_b_re': nrm(20, (L, 2, S5_GROUPS, S5_STATE, S5_GROUP), (2.0 * S5_GROUP) ** -0.5),
        's5_b_im': nrm(21, (L, 2, S5_GROUPS, S5_STATE, S5_GROUP), (2.0 * S5_GROUP) ** -0.5),
        's5_c_re': nrm(22, (L, 2, S5_GROUPS, S5_GROUP, S5_STATE), (2.0 * S5_STATE) ** -0.5),
        's5_c_im': nrm(23, (L, 2, S5_GROUPS, S5_GROUP, S5_STATE), (2.0 * S5_STATE) ** -0.5),
        's5_d': nrm(24, (L, S5_W), 0.5),
        's5_glu_w': nrm(25, (L, S5_W, S5_W), S5_W ** -0.5),
        's5_glu_b': nrm(26, (L, S5_W), 0.01),
        'moe_router_g': nrm(27, (L, D_MODEL, N_GROUPS), D_MODEL ** -0.5),
        'moe_router_g_b': nrm(28, (L, N_GROUPS), 0.01),
        'moe_router_e': nrm(29, (L, N_GROUPS, D_MODEL, EXPERTS_PER_GROUP), D_MODEL ** -0.5),
        'moe_router_e_b': nrm(30, (L, N_GROUPS, EXPERTS_PER_GROUP), 0.01),
        'moe_w1': nrm(31, (L, N_EXPERTS, D_MODEL, EXPERT_FF), D_MODEL ** -0.5),
        'moe_w3': nrm(32, (L, N_EXPERTS, D_MODEL, EXPERT_FF), D_MODEL ** -0.5),
        'moe_w2': nrm(33, (L, N_EXPERTS, EXPERT_FF, D_MODEL), beta * EXPERT_FF ** -0.5),
        'ple_proj': nrm(34, (L, PLE_DIM, D_MODEL), beta * PLE_DIM ** -0.5),
        'ple_gate': nrm(35, (L, D_MODEL, D_MODEL), D_MODEL ** -0.5),
        'ln_w': 1.0 + nrm(36, (L, 3, D_MODEL), 0.05),
        'ln_b': nrm(37, (L, 3, D_MODEL), 0.01),
    }


def reference(x, p, positions, w_in, w_out, rwkv_mu_prev, rwkv_mu_next, rwkv_w0, rwkv_w_up,
              rwkv_a0, rwkv_a_up, rwkv_g_up, rwkv_k_k, rwkv_k_a, rwkv_r_k, rwkv_lnx_w, rwkv_lnx_b,
              s5_lam_re, s5_lam_im, s5_log_dt, s5_b_re, s5_b_im, s5_c_re, s5_c_im, s5_d,
              s5_glu_w, s5_glu_b, moe_router_g, moe_router_g_b, moe_router_e, moe_router_e_b,
              moe_w1, moe_w3, moe_w2, ple_proj, ple_gate, ln_w, ln_b):
    alpha = (2.0 * DEPTH) ** 0.25
    for i in range(DEPTH):
        z = x @ w_in[i]
        z_ret, z_rwkv, z_s5 = jnp.split(z, [RET_COLS, RET_COLS + RWKV_COLS], axis=-1)
        y_ret = _retention_mixer(z_ret, positions)
        y_rwkv = _rwkv7_mixer(z_rwkv, rwkv_mu_prev[i], rwkv_mu_next[i], rwkv_w0[i], rwkv_w_up[i],
                              rwkv_a0[i], rwkv_a_up[i], rwkv_g_up[i], rwkv_k_k[i], rwkv_k_a[i],
                              rwkv_r_k[i], rwkv_lnx_w[i], rwkv_lnx_b[i])
        y_s5 = _s5_mixer(z_s5, s5_lam_re[i], s5_lam_im[i], s5_log_dt[i], s5_b_re[i], s5_b_im[i],
                         s5_c_re[i], s5_c_im[i], s5_d[i], s5_glu_w[i], s5_glu_b[i])
        mix = jnp.concatenate([y_ret, y_rwkv, y_s5], axis=-1).astype(x.dtype) @ w_out[i]
        x = _layernorm(alpha * x + mix, ln_w[i, 0], ln_b[i, 0])
        moe = _hier_moe(x, moe_router_g[i], moe_router_g_b[i], moe_router_e[i], moe_router_e_b[i],
                        moe_w1[i], moe_w3[i], moe_w2[i])
        x = _layernorm(alpha * x + moe, ln_w[i, 1], ln_b[i, 1])
        ple = jax.nn.sigmoid(x @ ple_gate[i]) * (p[i] @ ple_proj[i])
        x = _layernorm(alpha * x + ple, ln_w[i, 2], ln_b[i, 2])
    return x
```

```python
import functools
import math

import jax
import jax.numpy as jnp
import numpy as np
from jax import lax
from jax.experimental import pallas as pl
from jax.experimental.pallas import tpu as pltpu

F32 = jnp.float32
BF16 = jnp.bfloat16
HIGHEST = lax.Precision.HIGHEST

LANES = 128
VMEM_LIMIT = 56 * 1024 * 1024

RET_HD = 128
RET_CHUNK = 128
ROPE_BASE = 10000.0
RWKV_HD = 64
RWKV_CHUNK = 64
DECAY_LORA = 64
AAA_LORA = 64
GATE_LORA = 128
S5_GROUP = 16
S5_STATE = 64
N_GROUPS = 4
EXPERTS_PER_GROUP = 4
LN_EPS = 1e-5
RWKV_LNX_EPS = 64e-5


def _params(*sem):
    return pltpu.CompilerParams(dimension_semantics=sem, vmem_limit_bytes=VMEM_LIMIT)


def _dot(a, b):
    return jnp.dot(a.astype(BF16), b.astype(BF16), preferred_element_type=F32)


def _dot_nt(a, b):
    return lax.dot_general(a.astype(BF16), b.astype(BF16), (((1,), (1,)), ((), ())),
                           preferred_element_type=F32)


def _dot_tn(a, b):
    return lax.dot_general(a.astype(BF16), b.astype(BF16), (((0,), (0,)), ((), ())),
                           preferred_element_type=F32)


def _dot32(a, b):
    return jnp.dot(a, b, precision=HIGHEST, preferred_element_type=F32)


def _dot32_nt(a, b):
    return lax.dot_general(a, b, (((1,), (1,)), ((), ())), precision=HIGHEST,
                           preferred_element_type=F32)


def _dot32_tn(a, b):
    return lax.dot_general(a, b, (((0,), (0,)), ((), ())), precision=HIGHEST,
                           preferred_element_type=F32)


def _layernorm(h, w, b):
    mu = jnp.mean(h, axis=-1, keepdims=True)
    d = h - mu
    var = jnp.mean(d * d, axis=-1, keepdims=True)
    return d * lax.rsqrt(var + LN_EPS) * w + b


def _matmul_kernel(x_ref, w_ref, o_ref):
    o_ref[...] = jnp.dot(x_ref[...], w_ref[...], preferred_element_type=F32)


def _matmul(xb, wb, tm, tn):
    T, K = xb.shape
    N = wb.shape[1]
    tm = min(tm, T)
    return pl.pallas_call(
        _matmul_kernel,
        grid=(N // tn, T // tm),
        in_specs=[pl.BlockSpec((tm, K), lambda j, i: (i, 0)),
                  pl.BlockSpec((K, tn), lambda j, i: (0, j))],
        out_specs=pl.BlockSpec((tm, tn), lambda j, i: (i, j)),
        out_shape=jax.ShapeDtypeStruct((T, N), F32),
        compiler_params=_params("parallel", "parallel"),
        name="in_proj",
    )(xb, wb)


def _rope_table_kernel(pos_ref, inv_ref, sign_ref, cos_ref, sin_ref):
    ang = pos_ref[...].astype(F32) * inv_ref[...]
    cos_ref[...] = jnp.cos(ang)
    sin_ref[...] = jnp.sin(ang) * sign_ref[...]


def _rope_tables(positions):
    T = positions.size
    half = RET_HD // 2
    inv = ROPE_BASE ** (-jnp.arange(half, dtype=F32) / half)
    inv2 = jnp.concatenate([inv, inv])[None, :]
    sign = jnp.concatenate([-jnp.ones((half,), F32), jnp.ones((half,), F32)])[None, :]
    tm = min(1024, T)
    row = pl.BlockSpec((1, RET_HD), lambda i: (0, 0))
    out = pl.BlockSpec((tm, RET_HD), lambda i: (i, 0))
    return pl.pallas_call(
        _rope_table_kernel,
        grid=(T // tm,),
        in_specs=[pl.BlockSpec((tm, 1), lambda i: (i, 0)), row, row],
        out_specs=[out, out],
        out_shape=[jax.ShapeDtypeStruct((T, RET_HD), F32)] * 2,
        compiler_params=_params("parallel"),
        name="rope_tables",
    )(positions.reshape(T, 1), inv2, sign)


def _rope(t, cos2, sin2):
    return t * cos2 + pltpu.roll(t, RET_HD // 2, 1) * sin2


def _ret_bwd_kernel(k_ref, v_ref, cos_ref, sin_ref, dv_ref, dc_ref, sb_ref, state, *, heads):
    @pl.when(pl.program_id(1) == 0)
    def _():
        state[...] = jnp.zeros_like(state)

    cos2, sin2 = cos_ref[...], sin_ref[...]
    for h in range(heads):
        sl = slice(h * RET_HD, (h + 1) * RET_HD)
        k = _rope(k_ref[:, sl], cos2, sin2)
        s = state[h]
        sb_ref[0, 0, h] = s.astype(BF16)
        state[h] = dc_ref[h] * s + _dot_tn(k * dv_ref[h, 1], v_ref[:, sl])


def _ret_fwd_kernel(q_ref, k_ref, v_ref, g_ref, cos_ref, sin_ref, dm_ref, dv_ref, dc_ref, sb_ref,
                    o_ref, state, *, heads):
    @pl.when(pl.program_id(1) == 0)
    def _():
        state[...] = jnp.zeros_like(state)

    cos2, sin2 = cos_ref[...], sin_ref[...]
    for h in range(heads):
        sl = slice(h * RET_HD, (h + 1) * RET_HD)
        q = _rope(q_ref[:, sl], cos2, sin2) * (RET_HD ** -0.5)
        k = _rope(k_ref[:, sl], cos2, sin2)
        v = v_ref[:, sl]
        s = state[h]
        y = _dot(_dot_nt(q, k) * dm_ref[h], v)
        y = y + _dot(q * dv_ref[h, 2], s)
        y = y + jnp.dot((q * dv_ref[h, 3]).astype(BF16), sb_ref[0, 0, h], preferred_element_type=F32)
        state[h] = dc_ref[h] * s + _dot_tn(k * dv_ref[h, 0], v)
        y = y * lax.rsqrt(jnp.mean(y * y, axis=-1, keepdims=True) + 1e-6)
        g = g_ref[:, sl]
        o_ref[:, sl] = (y * (g * jax.nn.sigmoid(g))).astype(BF16)


def _retention(z_ret, cos2, sin2, batch):
    T, W4 = z_ret.shape
    W = W4 // 4
    H, C, Dh = W // RET_HD, RET_CHUNK, RET_HD
    nC = T // batch // C
    lg = jnp.log(1.0 - 2.0 ** (-5.0 - jnp.arange(H, dtype=F32)))
    pos = jnp.arange(C, dtype=F32)
    dmat = jnp.exp(lg[:, None, None] * jnp.abs(pos[:, None] - pos[None, :]))
    dvec = jnp.stack([jnp.exp(lg[:, None] * (C - 1 - pos)), jnp.exp(lg[:, None] * pos),
                      jnp.exp(lg[:, None] * (pos + 1.0)), jnp.exp(lg[:, None] * (C - pos))], axis=1)
    dvec = jnp.broadcast_to(dvec[..., None], (H, 4, C, Dh))
    dcay = jnp.broadcast_to(jnp.exp(lg * C)[:, None, None], (H, 1, Dh))

    def rows(f):
        return lambda b, n: (b * nC + f(n), 0)

    def cols(f, j):
        return lambda b, n: (b * nC + f(n), j)

    rev = lambda n: nC - 1 - n
    same = lambda n: n
    const3 = lambda b, n: (0, 0, 0)
    const4 = lambda b, n: (0, 0, 0, 0)
    sb = pl.pallas_call(
        functools.partial(_ret_bwd_kernel, heads=H),
        grid=(batch, nC),
        in_specs=[pl.BlockSpec((C, W), cols(rev, 1)), pl.BlockSpec((C, W), cols(rev, 2)),
                  pl.BlockSpec((C, Dh), rows(rev)), pl.BlockSpec((C, Dh), rows(rev)),
                  pl.BlockSpec((H, 4, C, Dh), const4), pl.BlockSpec((H, 1, Dh), const3)],
        out_specs=pl.BlockSpec((1, 1, H, Dh, Dh), lambda b, n: (b, nC - 1 - n, 0, 0, 0)),
        out_shape=jax.ShapeDtypeStruct((batch, nC, H, Dh, Dh), BF16),
        scratch_shapes=[pltpu.VMEM((H, Dh, Dh), F32)],
        compiler_params=_params("parallel", "arbitrary"),
        name="ret_bwd",
    )(z_ret, z_ret, cos2, sin2, dvec, dcay)
    return pl.pallas_call(
        functools.partial(_ret_fwd_kernel, heads=H),
        grid=(batch, nC),
        in_specs=[pl.BlockSpec((C, W), cols(same, 0)), pl.BlockSpec((C, W), cols(same, 1)),
                  pl.BlockSpec((C, W), cols(same, 2)), pl.BlockSpec((C, W), cols(same, 3)),
                  pl.BlockSpec((C, Dh), rows(same)), pl.BlockSpec((C, Dh), rows(same)),
                  pl.BlockSpec((H, C, C), const3), pl.BlockSpec((H, 4, C, Dh), const4),
                  pl.BlockSpec((H, 1, Dh), const3),
                  pl.BlockSpec((1, 1, H, Dh, Dh), lambda b, n: (b, n, 0, 0, 0))],
        out_specs=pl.BlockSpec((C, W), rows(same)),
        out_shape=jax.ShapeDtypeStruct((T, W), BF16),
        scratch_shapes=[pltpu.VMEM((H, Dh, Dh), F32)],
        compiler_params=_params("parallel", "arbitrary"),
        name="ret_fwd",
    )(z_ret, z_ret, z_ret, z_ret, cos2, sin2, dmat, dvec, dcay, sb)


def _head_sum(x, e_ref):
    n = e_ref.shape[0]
    parts = [_dot32(x[:, j:j + n], e_ref[...]) for j in range(0, x.shape[1], n)]
    return jnp.concatenate(parts, axis=1)


def _rwkv_pre_kernel(z_ref, zp_ref, zn_ref, mup_ref, mun_ref, w0_ref, wup_ref, a0_ref, aup_ref, gup_ref,
                     kk_w_ref, ka_ref, rk_ref, e_ref,
                     r_out, v_out, kk_out, kd_out, bd_out, lw_out, g_out, bonus_out, *, tiles_per_seq, W):
    i = pl.program_id(0)
    z = z_ref[...]
    tm = z.shape[0]
    row = lax.broadcasted_iota(jnp.int32, z.shape, 0)
    first = (i % tiles_per_seq) == 0
    last = (i % tiles_per_seq) == tiles_per_seq - 1
    prev_row = jnp.where(first, 0.0, zp_ref[7:8, :])
    next_row = jnp.where(last, 0.0, zn_ref[0:1, :])
    z_prev = jnp.where(row == 0, prev_row, pltpu.roll(z, 1, 0))
    z_next = jnp.where(row == tm - 1, next_row, pltpu.roll(z, tm - 1, 0))
    zf = z + mup_ref[...] * (z_prev - z) + mun_ref[...] * (z_next - z)
    r, k, v = zf[:, 0:W], zf[:, W:2 * W], zf[:, 2 * W:3 * W]
    o = 3 * W
    wd = jnp.tanh(zf[:, o:o + 2 * DECAY_LORA])
    ad = zf[:, o + 2 * DECAY_LORA:o + 2 * DECAY_LORA + 2 * AAA_LORA]
    gd = jax.nn.sigmoid(zf[:, o + 2 * DECAY_LORA + 2 * AAA_LORA:])
    kk = k * kk_w_ref[...]
    kk = kk / jnp.maximum(jnp.sqrt(_head_sum(kk * kk, e_ref)), 1e-12)
    r_out[...] = r
    v_out[...] = v
    kk_out[...] = kk
    g_out[...] = _dot(gd, gup_ref[...])
    bonus_out[...] = _head_sum(r * k * rk_ref[...], e_ref) * v
    for d in range(2):
        w_raw = w0_ref[d] + _dot(wd, wup_ref[d])
        lw_out[d] = (-math.exp(-0.5)) * jax.nn.sigmoid(w_raw)
        a = jax.nn.sigmoid(a0_ref[d] + _dot(ad, aup_ref[d]))
        kd_out[d] = k * (1.0 + (a - 1.0) * ka_ref[...])
        bd_out[d] = kk * a


def _rwkv_scan_kernel(r_ref, v_ref, kk_ref, kd_ref, bd_ref, lw_ref, y_ref, state, *, L, npairs):
    d = pl.program_id(0)

    @pl.when(pl.program_id(2) == 0)
    def _():
        state[...] = jnp.zeros_like(state)

    sgn = jnp.where(d == 0, 1, -1)
    row = lax.broadcasted_iota(jnp.int32, (L, L), 0) * sgn
    col = lax.broadcasted_iota(jnp.int32, (L, L), 1) * sgn
    tri = jnp.where(col <= row, 1.0, 0.0).astype(F32)
    lanec = lax.broadcasted_iota(jnp.int32, (L, 2 * L), 1)
    head_a = lanec < L
    rowc = lax.broadcasted_iota(jnp.int32, (L, 2 * L), 0) * sgn
    colc = jnp.where(head_a, lanec, lanec - L) * sgn
    strict = colc < rowc
    incl = colc <= rowc
    eye_cat = jnp.where(colc == rowc, 1.0, 0.0).astype(F32)
    r2 = lax.broadcasted_iota(jnp.int32, (2 * L, 2 * L), 0)
    c2 = lax.broadcasted_iota(jnp.int32, (2 * L, 2 * L), 1)
    same_head = jnp.where(r2 < L, 0, 1) == jnp.where(c2 < L, 0, 1)

    def bd(x):
        return jnp.concatenate([jnp.where(head_a, x, 0.0), jnp.where(head_a, 0.0, x)], axis=0)

    lw = lw_ref[0]
    cs = _dot32(tri, lw)
    ctot = jnp.sum(lw, axis=0, keepdims=True)
    for p in range(npairs):
        sl = slice(p * 2 * L, (p + 1) * 2 * L)
        c_in = cs[:, sl]
        c_ex = c_in - lw[:, sl]
        einv = jnp.exp(-c_in)
        wend = jnp.exp(ctot[:, sl] - c_in)
        b, kd, vp = bd_ref[0, :, sl], kd_ref[0, :, sl], v_ref[:, sl]
        kq = kk_ref[:, sl] * jnp.exp(c_ex)
        rq = r_ref[:, sl] * jnp.exp(c_in)
        bdec, kdec = b * einv, kd * einv
        lhs = jnp.concatenate([kq, rq], axis=0)
        g = _dot32_nt(lhs, jnp.concatenate([bd(bdec), bd(kdec)], axis=0))
        a_ab = jnp.where(strict, g[:L, :2 * L], 0.0)
        a_ak = jnp.where(strict, g[:L, 2 * L:], 0.0)
        a_rb = jnp.where(incl, g[L:, :2 * L], 0.0)
        a_rk = jnp.where(incl, g[L:, 2 * L:], 0.0)
        inv = eye_cat - a_ab
        apow = a_ab
        for _ in range(int(math.log2(L)) - 1):
            apow = _dot32(apow, bd(apow))
            inv = inv + _dot32(inv, bd(apow))
        s = state[p]
        kr = _dot32_nt(lhs, s)
        u = _dot32(inv, bd(-(kr[:L] + _dot32(a_ak, bd(vp)))))
        y = kr[L:] + _dot32(jnp.concatenate([a_rb, a_rk], axis=1), jnp.concatenate([bd(u), bd(vp)], axis=0))
        y_ref[0, :, sl] = y
        upd = _dot32_tn(jnp.concatenate([u, vp], axis=0), jnp.concatenate([b * wend, kd * wend], axis=0))
        state[p] = s * jnp.exp(ctot[:, sl]) + jnp.where(same_head, upd, 0.0)


def _rwkv_post_kernel(y_ref, g_ref, bonus_ref, lw_ref, lb_ref, e_ref, o_ref):
    y = y_ref[0] + y_ref[1]
    mu = _head_sum(y, e_ref) * (1.0 / RWKV_HD)
    dlt = y - mu
    var = _head_sum(dlt * dlt, e_ref) * (1.0 / RWKV_HD)
    yn = dlt * lax.rsqrt(var + RWKV_LNX_EPS) * lw_ref[...] + lb_ref[...]
    o_ref[...] = ((yn + bonus_ref[...]) * g_ref[...]).astype(BF16)


def _rwkv7(z, batch, mu_prev, mu_next, w0, w_up, a0, a_up, g_up, k_k, k_a, r_k, lnx_w, lnx_b):
    T, cols = z.shape
    W = w0.shape[-1]
    S = T // batch
    L = RWKV_CHUNK
    nC = S // L
    row = lambda x: x.reshape(1, -1).astype(F32)
    wup = jnp.zeros((2, 2 * DECAY_LORA, W), F32)
    wup = wup.at[0, :DECAY_LORA].set(w_up[0]).at[1, DECAY_LORA:].set(w_up[1]).astype(BF16)
    aup = jnp.zeros((2, 2 * AAA_LORA, W), F32)
    aup = aup.at[0, :AAA_LORA].set(a_up[0]).at[1, AAA_LORA:].set(a_up[1]).astype(BF16)
    eb = 4 * RWKV_HD
    hid = jnp.arange(eb) // RWKV_HD
    e = (hid[:, None] == hid[None, :]).astype(F32)
    tm = min(256, S)
    nt = T // tm
    full = lambda shape: pl.BlockSpec(shape, lambda i: (0,) * len(shape))
    tile = pl.BlockSpec((tm, W), lambda i: (i, 0))
    tile2 = pl.BlockSpec((2, tm, W), lambda i: (0, i, 0))
    hb = tm // 8
    r, v, kk, kd, bd, lw, g, bonus = pl.pallas_call(
        functools.partial(_rwkv_pre_kernel, tiles_per_seq=S // tm, W=W),
        grid=(nt,),
        in_specs=[pl.BlockSpec((tm, cols), lambda i: (i, 0)),
                  pl.BlockSpec((8, cols), lambda i: (jnp.maximum(i * hb - 1, 0), 0)),
                  pl.BlockSpec((8, cols), lambda i: (jnp.minimum((i + 1) * hb, T // 8 - 1), 0)),
                  full((1, cols)), full((1, cols)), full((2, 1, W)), full((2, 2 * DECAY_LORA, W)),
                  full((2, 1, W)), full((2, 2 * AAA_LORA, W)), full((GATE_LORA, W)),
                  full((1, W)), full((1, W)), full((1, W)), full((eb, eb))],
        out_specs=[tile, tile, tile, tile2, tile2, tile2, tile, tile],
        out_shape=[jax.ShapeDtypeStruct((T, W), F32)] * 3 + [jax.ShapeDtypeStruct((2, T, W), F32)] * 3
        + [jax.ShapeDtypeStruct((T, W), F32)] * 2,
        compiler_params=_params("parallel"),
        name="rwkv_pre",
    )(z, z, z, row(mu_prev), row(mu_next), w0.reshape(2, 1, W), wup, a0.reshape(2, 1, W), aup,
      g_up.astype(BF16), row(k_k), row(k_a), row(r_k), e)

    chunk = lambda d, b, c: b * nC + c + d * (nC - 1 - 2 * c)
    seq = pl.BlockSpec((L, W), lambda d, b, c: (chunk(d, b, c), 0))
    seq2 = pl.BlockSpec((1, L, W), lambda d, b, c: (d, chunk(d, b, c), 0))
    y = pl.pallas_call(
        functools.partial(_rwkv_scan_kernel, L=L, npairs=W // (2 * L)),
        grid=(2, batch, nC),
        in_specs=[seq, seq, seq, seq2, seq2, seq2],
        out_specs=seq2,
        out_shape=jax.ShapeDtypeStruct((2, T, W), F32),
        scratch_shapes=[pltpu.VMEM((W // (2 * L), 2 * L, 2 * L), F32)],
        compiler_params=_params("parallel", "parallel", "arbitrary"),
        name="rwkv_scan",
    )(r, v, kk, kd, bd, lw)

    return pl.pallas_call(
        _rwkv_post_kernel,
        grid=(nt,),
        in_specs=[tile2, tile, tile, full((1, W)), full((1, W)), full((eb, eb))],
        out_specs=tile,
        out_shape=jax.ShapeDtypeStruct((T, W), BF16),
        compiler_params=_params("parallel"),
        name="rwkv_post",
    )(y, g, bonus, row(lnx_w), row(lnx_b), e)


def _s5_bu_kernel(u_ref, bre_ref, bim_ref, ore_ref, oim_ref):
    u = u_ref[...]
    for d in range(2):
        ore_ref[d] = _dot(u, bre_ref[d])
        oim_ref[d] = _dot(u, bim_ref[d])


def _s5_scan_kernel(bre_ref, bim_ref, lre_ref, lim_ref, xre_ref, xim_ref, st_re, st_im, *, tt, batch):
    d = pl.program_id(0)

    @pl.when(pl.program_id(1) == 0)
    def _():
        st_re[...] = jnp.zeros_like(st_re)
        st_im[...] = jnp.zeros_like(st_im)

    ar, ai = lre_ref[0], lim_ref[0]

    def body(s, carry):
        t = jnp.where(d == 0, s, tt - 1 - s)
        out = []
        for b in range(batch):
            xr, xi = carry[b]
            nr = ar * xr - ai * xi + bre_ref[0, b, t]
            ni = ar * xi + ai * xr + bim_ref[0, b, t]
            xre_ref[0, b, t] = nr
            xim_ref[0, b, t] = ni
            out.append((nr, ni))
        return tuple(out)

    init = tuple((st_re[b], st_im[b]) for b in range(batch))
    fin = lax.fori_loop(0, tt, body, init, unroll=8)
    for b in range(batch):
        st_re[b] = fin[b][0]
        st_im[b] = fin[b][1]


def _s5_out_kernel(xre0, xim0, xre1, xim1, u_ref, cre_ref, cim_ref, dsk_ref, gw_ref, gb_ref, o_ref):
    y = u_ref[...] * dsk_ref[...]
    y = y + _dot(xre0[0], cre_ref[0]) - _dot(xim0[0], cim_ref[0])
    y = y + _dot(xre1[0], cre_ref[1]) - _dot(xim1[0], cim_ref[1])
    zg = jax.nn.gelu(y)
    o_ref[...] = (zg * jax.nn.sigmoid(_dot(zg, gw_ref[...]) + gb_ref[...])).astype(BF16)


def _s5(u, batch, lam_re, lam_im, log_dt, b_re, b_im, c_re, c_im, d_skip, glu_w, glu_b):
    T, W = u.shape
    S = T // batch
    G, P, Hh = lam_re.shape[1], lam_re.shape[2], S5_GROUP
    GP = G * P
    dt = jnp.exp(log_dt)[..., None]
    mag = jnp.exp(lam_re * dt)
    ab_re, ab_im = mag * jnp.cos(lam_im * dt), mag * jnp.sin(lam_im * dt)
    den = lam_re * lam_re + lam_im * lam_im
    nr, ni = ab_re - 1.0, ab_im
    cr = (nr * lam_re + ni * lam_im) / den
    ci = (ni * lam_re - nr * lam_im) / den
    bb_re = cr[..., None] * b_re - ci[..., None] * b_im
    bb_im = cr[..., None] * b_im + ci[..., None] * b_re
    eye = jnp.eye(G, dtype=F32)
    to_in = lambda m: (eye[None, :, None, :, None] * m.transpose(0, 1, 3, 2)[:, :, :, None, :]
                       ).reshape(2, G * Hh, GP).astype(BF16)
    to_out = lambda m: (eye[None, :, None, :, None] * m.transpose(0, 1, 3, 2)[:, :, :, None, :]
                        ).reshape(2, GP, G * Hh).astype(BF16)
    bmat_re, bmat_im = to_in(bb_re), to_in(bb_im)
    cmat_re, cmat_im = to_out(c_re), to_out(c_im)
    rows = GP // LANES
    lam_r = ab_re.reshape(2, rows, LANES)
    lam_i = ab_im.reshape(2, rows, LANES)

    tm = min(512, T)
    nt = T // tm
    bu_re, bu_im = pl.pallas_call(
        _s5_bu_kernel,
        grid=(nt,),
        in_specs=[pl.BlockSpec((tm, W), lambda i: (i, 0)),
                  pl.BlockSpec((2, W, GP), lambda i: (0, 0, 0)), pl.BlockSpec((2, W, GP), lambda i: (0, 0, 0))],
        out_specs=[pl.BlockSpec((2, tm, GP), lambda i: (0, i, 0))] * 2,
        out_shape=[jax.ShapeDtypeStruct((2, T, GP), F32)] * 2,
        compiler_params=_params("parallel"),
        name="s5_bu",
    )(u, bmat_re, bmat_im)

    tt = min(128, S)
    nT = S // tt
    shp5 = (2, batch, S, rows, LANES)
    blk = pl.BlockSpec((1, batch, tt, rows, LANES), lambda d, i: (d, 0, i + d * (nT - 1 - 2 * i), 0, 0))
    lam_blk = pl.BlockSpec((1, rows, LANES), lambda d, i: (d, 0, 0))
    x_re, x_im = pl.pallas_call(
        functools.partial(_s5_scan_kernel, tt=tt, batch=batch),
        grid=(2, nT),
        in_specs=[blk, blk, lam_blk, lam_blk],
        out_specs=[blk, blk],
        out_shape=[jax.ShapeDtypeStruct(shp5, F32)] * 2,
        scratch_shapes=[pltpu.VMEM((batch, rows, LANES), F32)] * 2,
        compiler_params=_params("parallel", "arbitrary"),
        name="s5_scan",
    )(bu_re.reshape(shp5), bu_im.reshape(shp5), lam_r, lam_i)
    x_re = x_re.reshape(2, T, GP)
    x_im = x_im.reshape(2, T, GP)

    xb = lambda d: pl.BlockSpec((1, tm, GP), lambda i: (d, i, 0))
    full = lambda shape: pl.BlockSpec(shape, lambda i: (0,) * len(shape))
    return pl.pallas_call(
        _s5_out_kernel,
        grid=(nt,),
        in_specs=[xb(0), xb(0), xb(1), xb(1), pl.BlockSpec((tm, W), lambda i: (i, 0)),
                  full((2, GP, W)), full((2, GP, W)), full((1, W)), full((W, W)), full((1, W))],
        out_specs=pl.BlockSpec((tm, W), lambda i: (i, 0)),
        out_shape=jax.ShapeDtypeStruct((T, W), BF16),
        compiler_params=_params("parallel"),
        name="s5_out",
    )(x_re, x_im, x_re, x_im, u, cmat_re, cmat_im, d_skip.reshape(1, W), glu_w.astype(BF16),
      glu_b.reshape(1, W))


def _outproj_ln_kernel(x_ref, a_ref, b_ref, c_ref, wa_ref, wb_ref, wc_ref, lw_ref, lb_ref, o_ref, ob_ref,
                       *, alpha):
    mix = (jnp.dot(a_ref[...], wa_ref[...], preferred_element_type=F32)
           + jnp.dot(b_ref[...], wb_ref[...], preferred_element_type=F32)
           + jnp.dot(c_ref[...], wc_ref[...], preferred_element_type=F32))
    y = _layernorm(alpha * x_ref[...] + mix, lw_ref[...], lb_ref[...])
    o_ref[...] = y
    ob_ref[...] = y.astype(BF16)


def _outproj_ln(x, ya, yb, yc, w_out_b, ln_w, ln_b, alpha):
    T, D = x.shape
    wa, wb, wc = ya.shape[1], yb.shape[1], yc.shape[1]
    assert wa == wb and (wa + wb) % wc == 0
    tm = min(256, T)
    rowt = lambda w: pl.BlockSpec((tm, w), lambda i: (i, 0))
    full = lambda shape: pl.BlockSpec(shape, lambda i: (0,) * len(shape))
    return pl.pallas_call(
        functools.partial(_outproj_ln_kernel, alpha=alpha),
        grid=(T // tm,),
        in_specs=[rowt(D), rowt(wa), rowt(wb), rowt(wc),
                  pl.BlockSpec((wa, D), lambda i: (0, 0)), pl.BlockSpec((wb, D), lambda i: (1, 0)),
                  pl.BlockSpec((wc, D), lambda i: ((wa + wb) // wc, 0)), full((1, D)), full((1, D))],
        out_specs=[rowt(D), rowt(D)],
        out_shape=[jax.ShapeDtypeStruct((T, D), F32), jax.ShapeDtypeStruct((T, D), BF16)],
        compiler_params=_params("parallel"),
        name="outproj_ln",
    )(x, ya, yb, yc, w_out_b, w_out_b, w_out_b, ln_w.reshape(1, D), ln_b.reshape(1, D))


def _first_max(vals):
    best = vals[0]
    for v in vals[1:]:
        best = jnp.maximum(best, v)
    idx = jnp.full(best.shape, len(vals) - 1, jnp.int32)
    for j in range(len(vals) - 2, -1, -1):
        idx = jnp.where(vals[j] == best, j, idx)
    return best, idx


def _router_kernel(x_ref, w_ref, b_ref, gate_ref):
    logits = _dot32(x_ref[...], w_ref[...]) + b_ref[...]
    ng, ne = N_GROUPS, EXPERTS_PER_GROUP
    gl = [logits[:, j:j + 1] for j in range(ng)]
    gmax, _ = _first_max(gl)
    ex = [jnp.exp(v - gmax) for v in gl]
    den = ex[0]
    for v in ex[1:]:
        den = den + v
    prob = [v / den for v in ex]
    p_top, grp = _first_max(prob)
    el = []
    for e in range(ne):
        acc = jnp.zeros_like(p_top)
        for g in range(ng):
            acc = jnp.where(grp == g, logits[:, ng + g * ne + e:ng + g * ne + e + 1], acc)
        el.append(acc)
    v1, i1 = _first_max(el)
    el2 = [jnp.where(i1 == e, -jnp.inf, el[e]) for e in range(ne)]
    v2, i2 = _first_max(el2)
    e2 = jnp.exp(v2 - v1)
    w1 = p_top * (1.0 / (1.0 + e2))
    w2 = p_top * (e2 / (1.0 + e2))
    lane = lax.broadcasted_iota(jnp.int32, logits.shape, 1)
    gate_ref[...] = (jnp.where(lane == grp * ne + i1, w1, 0.0) + jnp.where(lane == grp * ne + i2, w2, 0.0))


def _router(x, router_g, router_g_b, router_e, router_e_b):
    T, D = x.shape
    ng, ne = N_GROUPS, EXPERTS_PER_GROUP
    w = jnp.concatenate([router_g, router_e.transpose(1, 0, 2).reshape(D, ng * ne)], axis=1)
    w = jnp.pad(w, ((0, 0), (0, LANES - w.shape[1])))
    b = jnp.concatenate([router_g_b, router_e_b.reshape(ng * ne)])
    b = jnp.pad(b, (0, LANES - b.shape[0])).reshape(1, LANES)
    tm = min(512, T)
    return pl.pallas_call(
        _router_kernel,
        grid=(T // tm,),
        in_specs=[pl.BlockSpec((tm, D), lambda i: (i, 0)), pl.BlockSpec((D, LANES), lambda i: (0, 0)),
                  pl.BlockSpec((1, LANES), lambda i: (0, 0))],
        out_specs=pl.BlockSpec((tm, LANES), lambda i: (i, 0)),
        out_shape=jax.ShapeDtypeStruct((T, LANES), F32),
        compiler_params=_params("parallel"),
        name="router",
    )(x, w, b)


def _moe_ln_kernel(x_ref, xb_ref, gate_ref, w1_ref, w3_ref, w2_ref, lw_ref, lb_ref, o_ref, ob_ref, acc,
                   *, alpha, n_experts):
    e = pl.program_id(1)

    @pl.when(e == 0)
    def _():
        acc[...] = jnp.zeros_like(acc)

    xb = xb_ref[...]
    h1 = jnp.dot(xb, w1_ref[0], preferred_element_type=F32)
    h3 = jnp.dot(xb, w3_ref[0], preferred_element_type=F32)
    h = (h1 * jax.nn.sigmoid(h1)) * h3
    gate = gate_ref[...]
    lane = lax.broadcasted_iota(jnp.int32, gate.shape, 1)
    ge = jnp.sum(jnp.where(lane == e, gate, 0.0), axis=1, keepdims=True)
    acc[...] += ge * _dot(h, w2_ref[0])

    @pl.when(e == n_experts - 1)
    def _():
        y = _layernorm(alpha * x_ref[...] + acc[...], lw_ref[...], lb_ref[...])
        o_ref[...] = y
        ob_ref[...] = y.astype(BF16)


def _moe_ln(x, xb, gate, w1b, w3b, w2b, ln_w, ln_b, alpha):
    T, D = x.shape
    E, _, F = w1b.shape
    tm = min(512, T)
    rowt = lambda w: pl.BlockSpec((tm, w), lambda i, e: (i, 0))
    full = lambda shape: pl.BlockSpec(shape, lambda i, e: (0,) * len(shape))
    return pl.pallas_call(
        functools.partial(_moe_ln_kernel, alpha=alpha, n_experts=E),
        grid=(T // tm, E),
        in_specs=[rowt(D), rowt(D), rowt(LANES),
                  pl.BlockSpec((1, D, F), lambda i, e: (e, 0, 0)), pl.BlockSpec((1, D, F), lambda i, e: (e, 0, 0)),
                  pl.BlockSpec((1, F, D), lambda i, e: (e, 0, 0)), full((1, D)), full((1, D))],
        out_specs=[rowt(D), rowt(D)],
        out_shape=[jax.ShapeDtypeStruct((T, D), F32), jax.ShapeDtypeStruct((T, D), BF16)],
        scratch_shapes=[pltpu.VMEM((tm, D), F32)],
        compiler_params=_params("parallel", "arbitrary"),
        name="moe_ln",
    )(x, xb, gate, w1b, w3b, w2b, ln_w.reshape(1, D), ln_b.reshape(1, D))


def _ple_ln_kernel(x_ref, xb_ref, p_ref, wg_ref, wp_ref, lw_ref, lb_ref, o_ref, ob_ref, *, alpha):
    gate = jax.nn.sigmoid(jnp.dot(xb_ref[...], wg_ref[...], preferred_element_type=F32))
    ple = gate * _dot(p_ref[...], wp_ref[...])
    y = _layernorm(alpha * x_ref[...] + ple, lw_ref[...], lb_ref[...])
    o_ref[...] = y
    ob_ref[...] = y.astype(BF16)


def _ple_ln(x, xb, p, wgb, wpb, ln_w, ln_b, alpha):
    T, D = x.shape
    Pd = p.shape[1]
    tm = min(256, T)
    rowt = lambda w: pl.BlockSpec((tm, w), lambda i: (i, 0))
    full = lambda shape: pl.BlockSpec(shape, lambda i: (0,) * len(shape))
    return pl.pallas_call(
        functools.partial(_ple_ln_kernel, alpha=alpha),
        grid=(T // tm,),
        in_specs=[rowt(D), rowt(D), rowt(Pd), full((D, D)), full((Pd, D)), full((1, D)), full((1, D))],
        out_specs=[rowt(D), rowt(D)],
        out_shape=[jax.ShapeDtypeStruct((T, D), F32), jax.ShapeDtypeStruct((T, D), BF16)],
        compiler_params=_params("parallel"),
        name="ple_ln",
    )(x, xb, p, wgb, wpb, ln_w.reshape(1, D), ln_b.reshape(1, D))


def kernel(x, p, positions, w_in, w_out, rwkv_mu_prev, rwkv_mu_next, rwkv_w0, rwkv_w_up, rwkv_a0, rwkv_a_up, rwkv_g_up, rwkv_k_k, rwkv_k_a, rwkv_r_k, rwkv_lnx_w, rwkv_lnx_b, s5_lam_re, s5_lam_im, s5_log_dt, s5_b_re, s5_b_im, s5_c_re, s5_c_im, s5_d, s5_glu_w, s5_glu_b, moe_router_g, moe_router_g_b, moe_router_e, moe_router_e_b, moe_w1, moe_w3, moe_w2, ple_proj, ple_gate, ln_w, ln_b):
    B, S, D = x.shape
    depth = w_in.shape[0]
    T = B * S
    alpha = (2.0 * depth) ** 0.25
    ret_w = rwkv_w = rwkv_w0.shape[-1]
    s5_w = s5_d.shape[-1]
    ret_cols = 4 * ret_w
    rwkv_cols = rwkv_mu_prev.shape[-1]

    cos2, sin2 = _rope_tables(positions)
    xf = x.reshape(T, D)
    xb = xf.astype(BF16)
    for i in range(depth):
        wi = w_in[i].astype(BF16)
        z_ret = _matmul(xb, wi[:, :ret_cols], 1024, 1024)
        z_rwkv = _matmul(xb, wi[:, ret_cols:ret_cols + rwkv_cols], 1024, rwkv_cols // 3)
        z_s5 = _matmul(xb, wi[:, ret_cols + rwkv_cols:], 1024, s5_w)
        y_ret = _retention(z_ret, cos2, sin2, B)
        y_rwkv = _rwkv7(z_rwkv, B, rwkv_mu_prev[i], rwkv_mu_next[i], rwkv_w0[i], rwkv_w_up[i], rwkv_a0[i],
                        rwkv_a_up[i], rwkv_g_up[i], rwkv_k_k[i], rwkv_k_a[i], rwkv_r_k[i], rwkv_lnx_w[i],
                        rwkv_lnx_b[i])
        y_s5 = _s5(z_s5, B, s5_lam_re[i], s5_lam_im[i], s5_log_dt[i], s5_b_re[i], s5_b_im[i], s5_c_re[i],
                   s5_c_im[i], s5_d[i], s5_glu_w[i], s5_glu_b[i])
        xf, xb = _outproj_ln(xf, y_ret, y_rwkv, y_s5, w_out[i].astype(BF16), ln_w[i, 0], ln_b[i, 0], alpha)
        gate = _router(xf, moe_router_g[i], moe_router_g_b[i], moe_router_e[i], moe_router_e_b[i])
        xf, xb = _moe_ln(xf, xb, gate, moe_w1[i].astype(BF16), moe_w3[i].astype(BF16), moe_w2[i].astype(BF16),
                         ln_w[i, 1], ln_b[i, 1], alpha)
        xf, xb = _ple_ln(xf, xb, p[i].reshape(T, -1), ple_gate[i].astype(BF16), ple_proj[i].astype(BF16),
                         ln_w[i, 2], ln_b[i, 2], alpha)
    return xf.reshape(B, S, D)
```

```python
import functools
import math

import jax
import jax.numpy as jnp
import numpy as np
from jax import lax
from jax.experimental import pallas as pl
from jax.experimental.pallas import tpu as pltpu

F32 = jnp.float32
BF16 = jnp.bfloat16
HIGHEST = lax.Precision.HIGHEST

LANES = 128
VMEM_LIMIT = 56 * 1024 * 1024

RET_HD = 128
RET_CHUNK = 128
ROPE_BASE = 10000.0
RWKV_HD = 64
RWKV_CHUNK = 64
RWKV_HEADS_PER_GROUP = 4
DECAY_LORA = 64
AAA_LORA = 64
GATE_LORA = 128
S5_GROUP = 16
S5_STATE = 64
N_GROUPS = 4
EXPERTS_PER_GROUP = 4
LN_EPS = 1e-5
RWKV_LNX_EPS = 64e-5


def _params(*sem):
    return pltpu.CompilerParams(dimension_semantics=sem, vmem_limit_bytes=VMEM_LIMIT)


def _dot(a, b):
    return jnp.dot(a.astype(BF16), b.astype(BF16), preferred_element_type=F32)


def _dot_nt(a, b):
    return lax.dot_general(a.astype(BF16), b.astype(BF16), (((1,), (1,)), ((), ())),
                           preferred_element_type=F32)


def _dot_tn(a, b):
    return lax.dot_general(a.astype(BF16), b.astype(BF16), (((0,), (0,)), ((), ())),
                           preferred_element_type=F32)


def _dot32(a, b):
    return jnp.dot(a, b, precision=HIGHEST, preferred_element_type=F32)


def _dot32_nt(a, b):
    return lax.dot_general(a, b, (((1,), (1,)), ((), ())), precision=HIGHEST,
                           preferred_element_type=F32)


def _dot32_tn(a, b):
    return lax.dot_general(a, b, (((0,), (0,)), ((), ())), precision=HIGHEST,
                           preferred_element_type=F32)


def _layernorm(h, w, b):
    mu = jnp.mean(h, axis=-1, keepdims=True)
    d = h - mu
    var = jnp.mean(d * d, axis=-1, keepdims=True)
    return d * lax.rsqrt(var + LN_EPS) * w + b


def _matmul_kernel(x_ref, w_ref, o_ref):
    o_ref[...] = jnp.dot(x_ref[...], w_ref[...], preferred_element_type=F32)


def _matmul(xb, wb, tm, tn):
    T, K = xb.shape
    N = wb.shape[1]
    tm = min(tm, T)
    return pl.pallas_call(
        _matmul_kernel,
        grid=(N // tn, T // tm),
        in_specs=[pl.BlockSpec((tm, K), lambda j, i: (i, 0)),
                  pl.BlockSpec((K, tn), lambda j, i: (0, j))],
        out_specs=pl.BlockSpec((tm, tn), lambda j, i: (i, j)),
        out_shape=jax.ShapeDtypeStruct((T, N), F32),
        compiler_params=_params("parallel", "parallel"),
        name="in_proj",
    )(xb, wb)


def _rope_table_kernel(pos_ref, inv_ref, sign_ref, cos_ref, sin_ref):
    ang = pos_ref[...].astype(F32) * inv_ref[...]
    cos_ref[...] = jnp.cos(ang)
    sin_ref[...] = jnp.sin(ang) * sign_ref[...]


def _rope_tables(positions):
    T = positions.size
    half = RET_HD // 2
    inv = ROPE_BASE ** (-jnp.arange(half, dtype=F32) / half)
    inv2 = jnp.concatenate([inv, inv])[None, :]
    sign = jnp.concatenate([-jnp.ones((half,), F32), jnp.ones((half,), F32)])[None, :]
    tm = min(1024, T)
    row = pl.BlockSpec((1, RET_HD), lambda i: (0, 0))
    out = pl.BlockSpec((tm, RET_HD), lambda i: (i, 0))
    return pl.pallas_call(
        _rope_table_kernel,
        grid=(T // tm,),
        in_specs=[pl.BlockSpec((tm, 1), lambda i: (i, 0)), row, row],
        out_specs=[out, out],
        out_shape=[jax.ShapeDtypeStruct((T, RET_HD), F32)] * 2,
        compiler_params=_params("parallel"),
        name="rope_tables",
    )(positions.reshape(T, 1), inv2, sign)


def _rope(t, cos2, sin2):
    return t * cos2 + pltpu.roll(t, RET_HD // 2, 1) * sin2


def _ret_bwd_kernel(k_ref, v_ref, cos_ref, sin_ref, dv_ref, dc_ref, sb_ref, state, *, heads):
    @pl.when(pl.program_id(1) == 0)
    def _():
        state[...] = jnp.zeros_like(state)

    cos2, sin2 = cos_ref[...], sin_ref[...]
    for h in range(heads):
        sl = slice(h * RET_HD, (h + 1) * RET_HD)
        k = _rope(k_ref[:, sl], cos2, sin2)
        s = state[h]
        sb_ref[0, 0, h] = s.astype(BF16)
        state[h] = dc_ref[h] * s + _dot_tn(k * dv_ref[h, 1], v_ref[:, sl])


def _ret_fwd_kernel(q_ref, k_ref, v_ref, g_ref, cos_ref, sin_ref, dm_ref, dv_ref, dc_ref, sb_ref,
                    o_ref, state, *, heads):
    @pl.when(pl.program_id(1) == 0)
    def _():
        state[...] = jnp.zeros_like(state)

    cos2, sin2 = cos_ref[...], sin_ref[...]
    for h in range(heads):
        sl = slice(h * RET_HD, (h + 1) * RET_HD)
        q = _rope(q_ref[:, sl], cos2, sin2) * (RET_HD ** -0.5)
        k = _rope(k_ref[:, sl], cos2, sin2)
        v = v_ref[:, sl]
        s = state[h]
        y = _dot(_dot_nt(q, k) * dm_ref[h], v)
        y = y + _dot(q * dv_ref[h, 2], s)
        y = y + jnp.dot((q * dv_ref[h, 3]).astype(BF16), sb_ref[0, 0, h], preferred_element_type=F32)
        state[h] = dc_ref[h] * s + _dot_tn(k * dv_ref[h, 0], v)
        y = y * lax.rsqrt(jnp.mean(y * y, axis=-1, keepdims=True) + 1e-6)
        g = g_ref[:, sl]
        o_ref[:, sl] = (y * (g * jax.nn.sigmoid(g))).astype(BF16)


def _retention(z_ret, cos2, sin2, batch):
    T, W4 = z_ret.shape
    W = W4 // 4
    H, C, Dh = W // RET_HD, RET_CHUNK, RET_HD
    nC = T // batch // C
    lg = jnp.log(1.0 - 2.0 ** (-5.0 - jnp.arange(H, dtype=F32)))
    pos = jnp.arange(C, dtype=F32)
    dmat = jnp.exp(lg[:, None, None] * jnp.abs(pos[:, None] - pos[None, :]))
    dvec = jnp.stack([jnp.exp(lg[:, None] * (C - 1 - pos)), jnp.exp(lg[:, None] * pos),
                      jnp.exp(lg[:, None] * (pos + 1.0)), jnp.exp(lg[:, None] * (C - pos))], axis=1)
    dvec = jnp.broadcast_to(dvec[..., None], (H, 4, C, Dh))
    dcay = jnp.broadcast_to(jnp.exp(lg * C)[:, None, None], (H, 1, Dh))

    def rows(f):
        return lambda b, n: (b * nC + f(n), 0)

    def cols(f, j):
        return lambda b, n: (b * nC + f(n), j)

    rev = lambda n: nC - 1 - n
    same = lambda n: n
    const3 = lambda b, n: (0, 0, 0)
    const4 = lambda b, n: (0, 0, 0, 0)
    sb = pl.pallas_call(
        functools.partial(_ret_bwd_kernel, heads=H),
        grid=(batch, nC),
        in_specs=[pl.BlockSpec((C, W), cols(rev, 1)), pl.BlockSpec((C, W), cols(rev, 2)),
                  pl.BlockSpec((C, Dh), rows(rev)), pl.BlockSpec((C, Dh), rows(rev)),
                  pl.BlockSpec((H, 4, C, Dh), const4), pl.BlockSpec((H, 1, Dh), const3)],
        out_specs=pl.BlockSpec((1, 1, H, Dh, Dh), lambda b, n: (b, nC - 1 - n, 0, 0, 0)),
        out_shape=jax.ShapeDtypeStruct((batch, nC, H, Dh, Dh), BF16),
        scratch_shapes=[pltpu.VMEM((H, Dh, Dh), F32)],
        compiler_params=_params("parallel", "arbitrary"),
        name="ret_bwd",
    )(z_ret, z_ret, cos2, sin2, dvec, dcay)
    return pl.pallas_call(
        functools.partial(_ret_fwd_kernel, heads=H),
        grid=(batch, nC),
        in_specs=[pl.BlockSpec((C, W), cols(same, 0)), pl.BlockSpec((C, W), cols(same, 1)),
                  pl.BlockSpec((C, W), cols(same, 2)), pl.BlockSpec((C, W), cols(same, 3)),
                  pl.BlockSpec((C, Dh), rows(same)), pl.BlockSpec((C, Dh), rows(same)),
                  pl.BlockSpec((H, C, C), const3), pl.BlockSpec((H, 4, C, Dh), const4),
                  pl.BlockSpec((H, 1, Dh), const3),
                  pl.BlockSpec((1, 1, H, Dh, Dh), lambda b, n: (b, n, 0, 0, 0))],
        out_specs=pl.BlockSpec((C, W), rows(same)),
        out_shape=jax.ShapeDtypeStruct((T, W), BF16),
        scratch_shapes=[pltpu.VMEM((H, Dh, Dh), F32)],
        compiler_params=_params("parallel", "arbitrary"),
        name="ret_fwd",
    )(z_ret, z_ret, z_ret, z_ret, cos2, sin2, dmat, dvec, dcay, sb)


def _head_sum(x, e_ref):
    n = e_ref.shape[0]
    parts = [_dot32(x[:, j:j + n], e_ref[...]) for j in range(0, x.shape[1], n)]
    return jnp.concatenate(parts, axis=1)


def _rwkv_pre_kernel(z_ref, zp_ref, zn_ref, mup_ref, mun_ref, w0_ref, wup_ref, a0_ref, aup_ref, gup_ref,
                     kk_w_ref, ka_ref, rk_ref, e_ref,
                     r_out, v_out, kk_out, kd_out, bd_out, lw_out, g_out, bonus_out, *, tiles_per_seq, W):
    i = pl.program_id(0)
    z = z_ref[...]
    tm = z.shape[0]
    row = lax.broadcasted_iota(jnp.int32, z.shape, 0)
    first = (i % tiles_per_seq) == 0
    last = (i % tiles_per_seq) == tiles_per_seq - 1
    prev_row = jnp.where(first, 0.0, zp_ref[7:8, :])
    next_row = jnp.where(last, 0.0, zn_ref[0:1, :])
    z_prev = jnp.where(row == 0, prev_row, pltpu.roll(z, 1, 0))
    z_next = jnp.where(row == tm - 1, next_row, pltpu.roll(z, tm - 1, 0))
    zf = z + mup_ref[...] * (z_prev - z) + mun_ref[...] * (z_next - z)
    r, k, v = zf[:, 0:W], zf[:, W:2 * W], zf[:, 2 * W:3 * W]
    o = 3 * W
    wd = jnp.tanh(zf[:, o:o + 2 * DECAY_LORA])
    ad = zf[:, o + 2 * DECAY_LORA:o + 2 * DECAY_LORA + 2 * AAA_LORA]
    gd = jax.nn.sigmoid(zf[:, o + 2 * DECAY_LORA + 2 * AAA_LORA:])
    kk = k * kk_w_ref[...]
    kk = kk / jnp.maximum(jnp.sqrt(_head_sum(kk * kk, e_ref)), 1e-12)
    r_out[...] = r
    v_out[...] = v
    kk_out[...] = kk
    g_out[...] = _dot(gd, gup_ref[...])
    bonus_out[...] = _head_sum(r * k * rk_ref[...], e_ref) * v
    for d in range(2):
        w_raw = w0_ref[d] + _dot(wd, wup_ref[d])
        lw_out[d] = (-math.exp(-0.5)) * jax.nn.sigmoid(w_raw)
        a = jax.nn.sigmoid(a0_ref[d] + _dot(ad, aup_ref[d]))
        kd_out[d] = k * (1.0 + (a - 1.0) * ka_ref[...])
        bd_out[d] = kk * a


def _split_dot(m, x):
    mb = m.astype(BF16)
    x1 = x.astype(BF16)
    r1 = x - x1.astype(F32)
    x2 = r1.astype(BF16)
    x3 = (r1 - x2.astype(F32)).astype(BF16)
    dot = lambda t: jnp.dot(mb, t, preferred_element_type=F32)
    return dot(x1) + dot(x2) + dot(x3)


def _rwkv_chunk(direction, sl, r_ref, v_ref, kk_ref, kd_ref, bd_ref, lw, cs, ctot, y_ref, state, *, L, hpg):
    gw = hpg * L
    sgn = 1 if direction == 0 else -1
    lane = lax.broadcasted_iota(jnp.int32, (L, gw), 1)
    rowc = lax.broadcasted_iota(jnp.int32, (L, gw), 0) * sgn
    colc = jnp.bitwise_and(lane, L - 1) * sgn
    strict = colc < rowc
    incl = colc <= rowc
    eye_cat = jnp.where(colc == rowc, 1.0, 0.0).astype(F32)
    shift = int(math.log2(L))
    head_of_lane = jnp.right_shift(lane, shift)
    r2 = jnp.right_shift(lax.broadcasted_iota(jnp.int32, (gw, gw), 0), shift)
    c2 = jnp.right_shift(lax.broadcasted_iota(jnp.int32, (gw, gw), 1), shift)
    same_head = r2 == c2

    def bd(x):
        return jnp.concatenate([jnp.where(head_of_lane == h, x, 0.0) for h in range(hpg)], axis=0).astype(BF16)

    c_in = cs[:, sl]
    c_ex = c_in - lw[:, sl]
    einv = jnp.exp(-c_in)
    wend = jnp.exp(ctot[:, sl] - c_in)
    b, kd, vp = bd_ref[0, :, sl], kd_ref[0, :, sl], v_ref[:, sl]
    kq = kk_ref[:, sl] * jnp.exp(c_ex)
    rq = r_ref[:, sl] * jnp.exp(c_in)
    lhs = jnp.concatenate([kq, rq], axis=0).astype(BF16)
    g = _dot_nt(lhs, jnp.concatenate([bd(b * einv), bd(kd * einv)], axis=0))
    s = state[...]
    kr = _dot_nt(lhs, s)
    yield
    a_ab = jnp.where(strict, g[:L, :gw], 0.0)
    a_ak = jnp.where(strict, g[:L, gw:], 0.0)
    a_rb = jnp.where(incl, g[L:, :gw], 0.0)
    a_rk = jnp.where(incl, g[L:, gw:], 0.0)
    bvp = bd(vp)
    rhs = kr[:L] + _dot(a_ak, bvp)
    inv = eye_cat - a_ab
    apow = a_ab
    for _ in range(shift - 1):
        apow = _dot(apow, bd(apow))
        yield
        inv = inv + _dot(inv, bd(apow))
    yield
    u = _dot(inv, bd(-rhs))
    yield
    y = kr[L:] + _dot(jnp.concatenate([a_rb, a_rk], axis=1), jnp.concatenate([bd(u), bvp], axis=0))
    y_ref[:, sl] = y
    upd = _dot_tn(jnp.concatenate([u, vp], axis=0), jnp.concatenate([b * wend, kd * wend], axis=0))
    state[...] = s * jnp.exp(ctot[:, sl]) + jnp.where(same_head, upd, 0.0)


def _rwkv_scan_kernel(rf_ref, vf_ref, kkf_ref, rb_ref, vb_ref, kkb_ref, kdf_ref, bdf_ref, lwf_ref,
                      kdb_ref, bdb_ref, lwb_ref, yf_ref, yb_ref, state, *, L, hpg, ngroups):
    @pl.when(pl.program_id(1) == 0)
    def _():
        state[...] = jnp.zeros_like(state)

    row = lax.broadcasted_iota(jnp.int32, (L, L), 0)
    col = lax.broadcasted_iota(jnp.int32, (L, L), 1)
    gw = hpg * L
    chains = []
    for direction, (r_ref, v_ref, kk_ref, kd_ref, bd_ref, lw_ref, y_ref) in enumerate(
            [(rf_ref, vf_ref, kkf_ref, kdf_ref, bdf_ref, lwf_ref, yf_ref),
             (rb_ref, vb_ref, kkb_ref, kdb_ref, bdb_ref, lwb_ref, yb_ref)]):
        tri = jnp.where(col <= row if direction == 0 else col >= row, 1.0, 0.0).astype(F32)
        lw = lw_ref[0]
        cs = _split_dot(tri, lw)
        ctot = jnp.sum(lw, axis=0, keepdims=True)
        for gi in range(ngroups):
            chains.append(_rwkv_chunk(direction, slice(gi * gw, (gi + 1) * gw), r_ref, v_ref, kk_ref, kd_ref,
                                      bd_ref, lw, cs, ctot, y_ref, state.at[direction, gi], L=L, hpg=hpg))
    while chains:
        alive = []
        for chain in chains:
            if next(chain, "done") != "done":
                alive.append(chain)
        chains = alive


def _rwkv_post_kernel(yf_ref, yb_ref, g_ref, bonus_ref, lw_ref, lb_ref, e_ref, o_ref):
    y = yf_ref[...] + yb_ref[...]
    mu = _head_sum(y, e_ref) * (1.0 / RWKV_HD)
    dlt = y - mu
    var = _head_sum(dlt * dlt, e_ref) * (1.0 / RWKV_HD)
    yn = dlt * lax.rsqrt(var + RWKV_LNX_EPS) * lw_ref[...] + lb_ref[...]
    o_ref[...] = ((yn + bonus_ref[...]) * g_ref[...]).astype(BF16)


def _rwkv7(z, batch, mu_prev, mu_next, w0, w_up, a0, a_up, g_up, k_k, k_a, r_k, lnx_w, lnx_b):
    T, cols = z.shape
    W = w0.shape[-1]
    S = T // batch
    L = RWKV_CHUNK
    nC = S // L
    row = lambda x: x.reshape(1, -1).astype(F32)
    wup = jnp.zeros((2, 2 * DECAY_LORA, W), F32)
    wup = wup.at[0, :DECAY_LORA].set(w_up[0]).at[1, DECAY_LORA:].set(w_up[1]).astype(BF16)
    aup = jnp.zeros((2, 2 * AAA_LORA, W), F32)
    aup = aup.at[0, :AAA_LORA].set(a_up[0]).at[1, AAA_LORA:].set(a_up[1]).astype(BF16)
    eb = 4 * RWKV_HD
    hid = jnp.arange(eb) // RWKV_HD
    e = (hid[:, None] == hid[None, :]).astype(F32)
    tm = min(256, S)
    nt = T // tm
    full = lambda shape: pl.BlockSpec(shape, lambda i: (0,) * len(shape))
    tile = pl.BlockSpec((tm, W), lambda i: (i, 0))
    tile2 = pl.BlockSpec((2, tm, W), lambda i: (0, i, 0))
    hb = tm // 8
    r, v, kk, kd, bd, lw, g, bonus = pl.pallas_call(
        functools.partial(_rwkv_pre_kernel, tiles_per_seq=S // tm, W=W),
        grid=(nt,),
        in_specs=[pl.BlockSpec((tm, cols), lambda i: (i, 0)),
                  pl.BlockSpec((8, cols), lambda i: (jnp.maximum(i * hb - 1, 0), 0)),
                  pl.BlockSpec((8, cols), lambda i: (jnp.minimum((i + 1) * hb, T // 8 - 1), 0)),
                  full((1, cols)), full((1, cols)), full((2, 1, W)), full((2, 2 * DECAY_LORA, W)),
                  full((2, 1, W)), full((2, 2 * AAA_LORA, W)), full((GATE_LORA, W)),
                  full((1, W)), full((1, W)), full((1, W)), full((eb, eb))],
        out_specs=[tile, tile, tile, tile2, tile2, tile2, tile, tile],
        out_shape=[jax.ShapeDtypeStruct((T, W), F32)] * 3 + [jax.ShapeDtypeStruct((2, T, W), F32)] * 3
        + [jax.ShapeDtypeStruct((T, W), F32)] * 2,
        compiler_params=_params("parallel"),
        name="rwkv_pre",
    )(z, z, z, row(mu_prev), row(mu_next), w0.reshape(2, 1, W), wup, a0.reshape(2, 1, W), aup,
      g_up.astype(BF16), row(k_k), row(k_a), row(r_k), e)

    hpg = RWKV_HEADS_PER_GROUP
    gw = hpg * L
    fw = lambda b, c: b * nC + c
    bw = lambda b, c: b * nC + nC - 1 - c
    seq = lambda f: pl.BlockSpec((L, W), lambda b, c: (f(b, c), 0))
    seq2 = lambda d, f: pl.BlockSpec((1, L, W), lambda b, c: (d, f(b, c), 0))
    y_f, y_b = pl.pallas_call(
        functools.partial(_rwkv_scan_kernel, L=L, hpg=hpg, ngroups=W // gw),
        grid=(batch, nC),
        in_specs=[seq(fw)] * 3 + [seq(bw)] * 3 + [seq2(0, fw)] * 3 + [seq2(1, bw)] * 3,
        out_specs=[seq(fw), seq(bw)],
        out_shape=[jax.ShapeDtypeStruct((T, W), F32)] * 2,
        scratch_shapes=[pltpu.VMEM((2, W // gw, gw, gw), F32)],
        compiler_params=_params("parallel", "arbitrary"),
        name="rwkv_scan",
    )(r, v, kk, r, v, kk, kd, bd, lw, kd, bd, lw)

    return pl.pallas_call(
        _rwkv_post_kernel,
        grid=(nt,),
        in_specs=[tile, tile, tile, tile, full((1, W)), full((1, W)), full((eb, eb))],
        out_specs=tile,
        out_shape=jax.ShapeDtypeStruct((T, W), BF16),
        compiler_params=_params("parallel"),
        name="rwkv_post",
    )(y_f, y_b, g, bonus, row(lnx_w), row(lnx_b), e)


def _s5_bu_kernel(u_ref, bre_ref, bim_ref, ore_ref, oim_ref):
    u = u_ref[...]
    for d in range(2):
        ore_ref[d] = _dot(u, bre_ref[d])
        oim_ref[d] = _dot(u, bim_ref[d])


def _s5_scan_kernel(bre_ref, bim_ref, lre_ref, lim_ref, xre_ref, xim_ref, st_re, st_im, *, tt, batch):
    d = pl.program_id(0)

    @pl.when(pl.program_id(1) == 0)
    def _():
        st_re[...] = jnp.zeros_like(st_re)
        st_im[...] = jnp.zeros_like(st_im)

    ar, ai = lre_ref[0], lim_ref[0]

    def body(s, carry):
        t = jnp.where(d == 0, s, tt - 1 - s)
        out = []
        for b in range(batch):
            xr, xi = carry[b]
            nr = ar * xr - ai * xi + bre_ref[0, b, t]
            ni = ar * xi + ai * xr + bim_ref[0, b, t]
            xre_ref[0, b, t] = nr
            xim_ref[0, b, t] = ni
            out.append((nr, ni))
        return tuple(out)

    init = tuple((st_re[b], st_im[b]) for b in range(batch))
    fin = lax.fori_loop(0, tt, body, init, unroll=8)
    for b in range(batch):
        st_re[b] = fin[b][0]
        st_im[b] = fin[b][1]


def _s5_out_kernel(xre0, xim0, xre1, xim1, u_ref, cre_ref, cim_ref, dsk_ref, gw_ref, gb_ref, o_ref):
    y = u_ref[...] * dsk_ref[...]
    y = y + _dot(xre0[0], cre_ref[0]) - _dot(xim0[0], cim_ref[0])
    y = y + _dot(xre1[0], cre_ref[1]) - _dot(xim1[0], cim_ref[1])
    zg = jax.nn.gelu(y)
    o_ref[...] = (zg * jax.nn.sigmoid(_dot(zg, gw_ref[...]) + gb_ref[...])).astype(BF16)


def _s5(u, batch, lam_re, lam_im, log_dt, b_re, b_im, c_re, c_im, d_skip, glu_w, glu_b):
    T, W = u.shape
    S = T // batch
    G, P, Hh = lam_re.shape[1], lam_re.shape[2], S5_GROUP
    GP = G * P
    dt = jnp.exp(log_dt)[..., None]
    mag = jnp.exp(lam_re * dt)
    ab_re, ab_im = mag * jnp.cos(lam_im * dt), mag * jnp.sin(lam_im * dt)
    den = lam_re * lam_re + lam_im * lam_im
    nr, ni = ab_re - 1.0, ab_im
    cr = (nr * lam_re + ni * lam_im) / den
    ci = (ni * lam_re - nr * lam_im) / den
    bb_re = cr[..., None] * b_re - ci[..., None] * b_im
    bb_im = cr[..., None] * b_im + ci[..., None] * b_re
    eye = jnp.eye(G, dtype=F32)
    to_in = lambda m: (eye[None, :, None, :, None] * m.transpose(0, 1, 3, 2)[:, :, :, None, :]
                       ).reshape(2, G * Hh, GP).astype(BF16)
    to_out = lambda m: (eye[None, :, None, :, None] * m.transpose(0, 1, 3, 2)[:, :, :, None, :]
                        ).reshape(2, GP, G * Hh).astype(BF16)
    bmat_re, bmat_im = to_in(bb_re), to_in(bb_im)
    cmat_re, cmat_im = to_out(c_re), to_out(c_im)
    rows = GP // LANES
    lam_r = ab_re.reshape(2, rows, LANES)
    lam_i = ab_im.reshape(2, rows, LANES)

    tm = min(512, T)
    nt = T // tm
    bu_re, bu_im = pl.pallas_call(
        _s5_bu_kernel,
        grid=(nt,),
        in_specs=[pl.BlockSpec((tm, W), lambda i: (i, 0)),
                  pl.BlockSpec((2, W, GP), lambda i: (0, 0, 0)), pl.BlockSpec((2, W, GP), lambda i: (0, 0, 0))],
        out_specs=[pl.BlockSpec((2, tm, GP), lambda i: (0, i, 0))] * 2,
        out_shape=[jax.ShapeDtypeStruct((2, T, GP), F32)] * 2,
        compiler_params=_params("parallel"),
        name="s5_bu",
    )(u, bmat_re, bmat_im)

    tt = min(128, S)
    nT = S // tt
    shp5 = (2, batch, S, rows, LANES)
    blk = pl.BlockSpec((1, batch, tt, rows, LANES), lambda d, i: (d, 0, i + d * (nT - 1 - 2 * i), 0, 0))
    lam_blk = pl.BlockSpec((1, rows, LANES), lambda d, i: (d, 0, 0))
    x_re, x_im = pl.pallas_call(
        functools.partial(_s5_scan_kernel, tt=tt, batch=batch),
        grid=(2, nT),
        in_specs=[blk, blk, lam_blk, lam_blk],
        out_specs=[blk, blk],
        out_shape=[jax.ShapeDtypeStruct(shp5, F32)] * 2,
        scratch_shapes=[pltpu.VMEM((batch, rows, LANES), F32)] * 2,
        compiler_params=_params("parallel", "arbitrary"),
        name="s5_scan",
    )(bu_re.reshape(shp5), bu_im.reshape(shp5), lam_r, lam_i)
    x_re = x_re.reshape(2, T, GP)
    x_im = x_im.reshape(2, T, GP)

    xb = lambda d: pl.BlockSpec((1, tm, GP), lambda i: (d, i, 0))
    full = lambda shape: pl.BlockSpec(shape, lambda i: (0,) * len(shape))
    return pl.pallas_call(
        _s5_out_kernel,
        grid=(nt,),
        in_specs=[xb(0), xb(0), xb(1), xb(1), pl.BlockSpec((tm, W), lambda i: (i, 0)),
                  full((2, GP, W)), full((2, GP, W)), full((1, W)), full((W, W)), full((1, W))],
        out_specs=pl.BlockSpec((tm, W), lambda i: (i, 0)),
        out_shape=jax.ShapeDtypeStruct((T, W), BF16),
        compiler_params=_params("parallel"),
        name="s5_out",
    )(x_re, x_im, x_re, x_im, u, cmat_re, cmat_im, d_skip.reshape(1, W), glu_w.astype(BF16),
      glu_b.reshape(1, W))


def _outproj_ln_kernel(x_ref, a_ref, b_ref, c_ref, wa_ref, wb_ref, wc_ref, lw_ref, lb_ref, o_ref, ob_ref,
                       *, alpha):
    mix = (jnp.dot(a_ref[...], wa_ref[...], preferred_element_type=F32)
           + jnp.dot(b_ref[...], wb_ref[...], preferred_element_type=F32)
           + jnp.dot(c_ref[...], wc_ref[...], preferred_element_type=F32))
    y = _layernorm(alpha * x_ref[...] + mix, lw_ref[...], lb_ref[...])
    o_ref[...] = y
    ob_ref[...] = y.astype(BF16)


def _outproj_ln(x, ya, yb, yc, w_out_b, ln_w, ln_b, alpha):
    T, D = x.shape
    wa, wb, wc = ya.shape[1], yb.shape[1], yc.shape[1]
    assert wa == wb and (wa + wb) % wc == 0
    tm = min(256, T)
    rowt = lambda w: pl.BlockSpec((tm, w), lambda i: (i, 0))
    full = lambda shape: pl.BlockSpec(shape, lambda i: (0,) * len(shape))
    return pl.pallas_call(
        functools.partial(_outproj_ln_kernel, alpha=alpha),
        grid=(T // tm,),
        in_specs=[rowt(D), rowt(wa), rowt(wb), rowt(wc),
                  pl.BlockSpec((wa, D), lambda i: (0, 0)), pl.BlockSpec((wb, D), lambda i: (1, 0)),
                  pl.BlockSpec((wc, D), lambda i: ((wa + wb) // wc, 0)), full((1, D)), full((1, D))],
        out_specs=[rowt(D), rowt(D)],
        out_shape=[jax.ShapeDtypeStruct((T, D), F32), jax.ShapeDtypeStruct((T, D), BF16)],
        compiler_params=_params("parallel"),
        name="outproj_ln",
    )(x, ya, yb, yc, w_out_b, w_out_b, w_out_b, ln_w.reshape(1, D), ln_b.reshape(1, D))


def _first_max(vals):
    best = vals[0]
    for v in vals[1:]:
        best = jnp.maximum(best, v)
    idx = jnp.full(best.shape, len(vals) - 1, jnp.int32)
    for j in range(len(vals) - 2, -1, -1):
        idx = jnp.where(vals[j] == best, j, idx)
    return best, idx


def _router_kernel(x_ref, w_ref, b_ref, gate_ref):
    logits = _dot32(x_ref[...], w_ref[...]) + b_ref[...]
    ng, ne = N_GROUPS, EXPERTS_PER_GROUP
    gl = [logits[:, j:j + 1] for j in range(ng)]
    gmax, _ = _first_max(gl)
    ex = [jnp.exp(v - gmax) for v in gl]
    den = ex[0]
    for v in ex[1:]:
        den = den + v
    prob = [v / den for v in ex]
    p_top, grp = _first_max(prob)
    el = []
    for e in range(ne):
        acc = jnp.zeros_like(p_top)
        for g in range(ng):
            acc = jnp.where(grp == g, logits[:, ng + g * ne + e:ng + g * ne + e + 1], acc)
        el.append(acc)
    v1, i1 = _first_max(el)
    el2 = [jnp.where(i1 == e, -jnp.inf, el[e]) for e in range(ne)]
    v2, i2 = _first_max(el2)
    e2 = jnp.exp(v2 - v1)
    w1 = p_top * (1.0 / (1.0 + e2))
    w2 = p_top * (e2 / (1.0 + e2))
    lane = lax.broadcasted_iota(jnp.int32, logits.shape, 1)
    gate_ref[...] = (jnp.where(lane == grp * ne + i1, w1, 0.0) + jnp.where(lane == grp * ne + i2, w2, 0.0))


def _router(x, router_g, router_g_b, router_e, router_e_b):
    T, D = x.shape
    ng, ne = N_GROUPS, EXPERTS_PER_GROUP
    w = jnp.concatenate([router_g, router_e.transpose(1, 0, 2).reshape(D, ng * ne)], axis=1)
    w = jnp.pad(w, ((0, 0), (0, LANES - w.shape[1])))
    b = jnp.concatenate([router_g_b, router_e_b.reshape(ng * ne)])
    b = jnp.pad(b, (0, LANES - b.shape[0])).reshape(1, LANES)
    tm = min(512, T)
    return pl.pallas_call(
        _router_kernel,
        grid=(T // tm,),
        in_specs=[pl.BlockSpec((tm, D), lambda i: (i, 0)), pl.BlockSpec((D, LANES), lambda i: (0, 0)),
                  pl.BlockSpec((1, LANES), lambda i: (0, 0))],
        out_specs=pl.BlockSpec((tm, LANES), lambda i: (i, 0)),
        out_shape=jax.ShapeDtypeStruct((T, LANES), F32),
        compiler_params=_params("parallel"),
        name="router",
    )(x, w, b)


def _moe_ln_kernel(x_ref, xb_ref, gate_ref, w1_ref, w3_ref, w2_ref, lw_ref, lb_ref, o_ref, ob_ref, acc,
                   *, alpha, n_experts):
    e = pl.program_id(1)

    @pl.when(e == 0)
    def _():
        acc[...] = jnp.zeros_like(acc)

    xb = xb_ref[...]
    h1 = jnp.dot(xb, w1_ref[0], preferred_element_type=F32)
    h3 = jnp.dot(xb, w3_ref[0], preferred_element_type=F32)
    h = (h1 * jax.nn.sigmoid(h1)) * h3
    gate = gate_ref[...]
    lane = lax.broadcasted_iota(jnp.int32, gate.shape, 1)
    ge = jnp.sum(jnp.where(lane == e, gate, 0.0), axis=1, keepdims=True)
    acc[...] += ge * _dot(h, w2_ref[0])

    @pl.when(e == n_experts - 1)
    def _():
        y = _layernorm(alpha * x_ref[...] + acc[...], lw_ref[...], lb_ref[...])
        o_ref[...] = y
        ob_ref[...] = y.astype(BF16)


def _moe_ln(x, xb, gate, w1b, w3b, w2b, ln_w, ln_b, alpha):
    T, D = x.shape
    E, _, F = w1b.shape
    tm = min(512, T)
    rowt = lambda w: pl.BlockSpec((tm, w), lambda i, e: (i, 0))
    full = lambda shape: pl.BlockSpec(shape, lambda i, e: (0,) * len(shape))
    return pl.pallas_call(
        functools.partial(_moe_ln_kernel, alpha=alpha, n_experts=E),
        grid=(T // tm, E),
        in_specs=[rowt(D), rowt(D), rowt(LANES),
                  pl.BlockSpec((1, D, F), lambda i, e: (e, 0, 0)), pl.BlockSpec((1, D, F), lambda i, e: (e, 0, 0)),
                  pl.BlockSpec((1, F, D), lambda i, e: (e, 0, 0)), full((1, D)), full((1, D))],
        out_specs=[rowt(D), rowt(D)],
        out_shape=[jax.ShapeDtypeStruct((T, D), F32), jax.ShapeDtypeStruct((T, D), BF16)],
        scratch_shapes=[pltpu.VMEM((tm, D), F32)],
        compiler_params=_params("parallel", "arbitrary"),
        name="moe_ln",
    )(x, xb, gate, w1b, w3b, w2b, ln_w.reshape(1, D), ln_b.reshape(1, D))


def _ple_ln_kernel(x_ref, xb_ref, p_ref, wg_ref, wp_ref, lw_ref, lb_ref, o_ref, ob_ref, *, alpha):
    gate = jax.nn.sigmoid(jnp.dot(xb_ref[...], wg_ref[...], preferred_element_type=F32))
    ple = gate * _dot(p_ref[...], wp_ref[...])
    y = _layernorm(alpha * x_ref[...] + ple, lw_ref[...], lb_ref[...])
    o_ref[...] = y
    ob_ref[...] = y.astype(BF16)


def _ple_ln(x, xb, p, wgb, wpb, ln_w, ln_b, alpha):
    T, D = x.shape
    Pd = p.shape[1]
    tm = min(256, T)
    rowt = lambda w: pl.BlockSpec((tm, w), lambda i: (i, 0))
    full = lambda shape: pl.BlockSpec(shape, lambda i: (0,) * len(shape))
    return pl.pallas_call(
        functools.partial(_ple_ln_kernel, alpha=alpha),
        grid=(T // tm,),
        in_specs=[rowt(D), rowt(D), rowt(Pd), full((D, D)), full((Pd, D)), full((1, D)), full((1, D))],
        out_specs=[rowt(D), rowt(D)],
        out_shape=[jax.ShapeDtypeStruct((T, D), F32), jax.ShapeDtypeStruct((T, D), BF16)],
        compiler_params=_params("parallel"),
        name="ple_ln",
    )(x, xb, p, wgb, wpb, ln_w.reshape(1, D), ln_b.reshape(1, D))


def kernel(x, p, positions, w_in, w_out, rwkv_mu_prev, rwkv_mu_next, rwkv_w0, rwkv_w_up, rwkv_a0, rwkv_a_up, rwkv_g_up, rwkv_k_k, rwkv_k_a, rwkv_r_k, rwkv_lnx_w, rwkv_lnx_b, s5_lam_re, s5_lam_im, s5_log_dt, s5_b_re, s5_b_im, s5_c_re, s5_c_im, s5_d, s5_glu_w, s5_glu_b, moe_router_g, moe_router_g_b, moe_router_e, moe_router_e_b, moe_w1, moe_w3, moe_w2, ple_proj, ple_gate, ln_w, ln_b):
    B, S, D = x.shape
    depth = w_in.shape[0]
    T = B * S
    alpha = (2.0 * depth) ** 0.25
    ret_w = rwkv_w = rwkv_w0.shape[-1]
    s5_w = s5_d.shape[-1]
    ret_cols = 4 * ret_w
    rwkv_cols = rwkv_mu_prev.shape[-1]

    cos2, sin2 = _rope_tables(positions)
    xf = x.reshape(T, D)
    xb = xf.astype(BF16)
    for i in range(depth):
        wi = w_in[i].astype(BF16)
        z_ret = _matmul(xb, wi[:, :ret_cols], 1024, 1024)
        z_rwkv = _matmul(xb, wi[:, ret_cols:ret_cols + rwkv_cols], 1024, rwkv_cols // 3)
        z_s5 = _matmul(xb, wi[:, ret_cols + rwkv_cols:], 1024, s5_w)
        y_ret = _retention(z_ret, cos2, sin2, B)
        y_rwkv = _rwkv7(z_rwkv, B, rwkv_mu_prev[i], rwkv_mu_next[i], rwkv_w0[i], rwkv_w_up[i], rwkv_a0[i],
                        rwkv_a_up[i], rwkv_g_up[i], rwkv_k_k[i], rwkv_k_a[i], rwkv_r_k[i], rwkv_lnx_w[i],
                        rwkv_lnx_b[i])
        y_s5 = _s5(z_s5, B, s5_lam_re[i], s5_lam_im[i], s5_log_dt[i], s5_b_re[i], s5_b_im[i], s5_c_re[i],
                   s5_c_im[i], s5_d[i], s5_glu_w[i], s5_glu_b[i])
        xf, xb = _outproj_ln(xf, y_ret, y_rwkv, y_s5, w_out[i].astype(BF16), ln_w[i, 0], ln_b[i, 0], alpha)
        gate = _router(xf, moe_router_g[i], moe_router_g_b[i], moe_router_e[i], moe_router_e_b[i])
        xf, xb = _moe_ln(xf, xb, gate, moe_w1[i].astype(BF16), moe_w3[i].astype(BF16), moe_w2[i].astype(BF16),
                         ln_w[i, 1], ln_b[i, 1], alpha)
        xf, xb = _ple_ln(xf, xb, p[i].reshape(T, -1), ple_gate[i].astype(BF16), ple_proj[i].astype(BF16),
                         ln_w[i, 2], ln_b[i, 2], alpha)
    return xf.reshape(B, S, D)
```

```python
import functools
import math

import jax
import jax.numpy as jnp
import numpy as np
from jax import lax
from jax.experimental import pallas as pl
from jax.experimental.pallas import tpu as pltpu

F32 = jnp.float32
BF16 = jnp.bfloat16
HIGHEST = lax.Precision.HIGHEST

LANES = 128
VMEM_LIMIT = 56 * 1024 * 1024

RET_HD = 128
RET_CHUNK = 128
ROPE_BASE = 10000.0
RWKV_HD = 64
RWKV_CHUNK = 64
RWKV_HEADS_PER_GROUP = 4
DECAY_LORA = 64
AAA_LORA = 64
GATE_LORA = 128
S5_GROUP = 16
S5_STATE = 64
S5_SUPER = 8
N_GROUPS = 4
EXPERTS_PER_GROUP = 4
MOE_CAP_NUM, MOE_CAP_DEN = 5, 16
LN_EPS = 1e-5
RWKV_LNX_EPS = 64e-5


def _params(*sem):
    return pltpu.CompilerParams(dimension_semantics=sem, vmem_limit_bytes=VMEM_LIMIT)


def _dot(a, b):
    return jnp.dot(a.astype(BF16), b.astype(BF16), preferred_element_type=F32)


def _dot_nt(a, b):
    return lax.dot_general(a.astype(BF16), b.astype(BF16), (((1,), (1,)), ((), ())),
                           preferred_element_type=F32)


def _dot_tn(a, b):
    return lax.dot_general(a.astype(BF16), b.astype(BF16), (((0,), (0,)), ((), ())),
                           preferred_element_type=F32)


def _dot32(a, b):
    return jnp.dot(a, b, precision=HIGHEST, preferred_element_type=F32)


def _dot32_nt(a, b):
    return lax.dot_general(a, b, (((1,), (1,)), ((), ())), precision=HIGHEST,
                           preferred_element_type=F32)


def _dot32_tn(a, b):
    return lax.dot_general(a, b, (((0,), (0,)), ((), ())), precision=HIGHEST,
                           preferred_element_type=F32)


def _layernorm(h, w, b):
    mu = jnp.mean(h, axis=-1, keepdims=True)
    d = h - mu
    var = jnp.mean(d * d, axis=-1, keepdims=True)
    return d * lax.rsqrt(var + LN_EPS) * w + b


def _matmul_kernel(x_ref, w_ref, o_ref):
    o_ref[...] = jnp.dot(x_ref[...], w_ref[...], preferred_element_type=F32)


def _matmul(xb, wb, tm, tn):
    T, K = xb.shape
    N = wb.shape[1]
    tm = min(tm, T)
    return pl.pallas_call(
        _matmul_kernel,
        grid=(N // tn, T // tm),
        in_specs=[pl.BlockSpec((tm, K), lambda j, i: (i, 0)),
                  pl.BlockSpec((K, tn), lambda j, i: (0, j))],
        out_specs=pl.BlockSpec((tm, tn), lambda j, i: (i, j)),
        out_shape=jax.ShapeDtypeStruct((T, N), F32),
        compiler_params=_params("parallel", "parallel"),
        name="in_proj",
    )(xb, wb)


def _rope_table_kernel(pos_ref, inv_ref, sign_ref, cos_ref, sin_ref):
    ang = pos_ref[...].astype(F32) * inv_ref[...]
    cos_ref[...] = jnp.cos(ang)
    sin_ref[...] = jnp.sin(ang) * sign_ref[...]


def _rope_tables(positions):
    T = positions.size
    half = RET_HD // 2
    inv = ROPE_BASE ** (-jnp.arange(half, dtype=F32) / half)
    inv2 = jnp.concatenate([inv, inv])[None, :]
    sign = jnp.concatenate([-jnp.ones((half,), F32), jnp.ones((half,), F32)])[None, :]
    tm = min(1024, T)
    row = pl.BlockSpec((1, RET_HD), lambda i: (0, 0))
    out = pl.BlockSpec((tm, RET_HD), lambda i: (i, 0))
    return pl.pallas_call(
        _rope_table_kernel,
        grid=(T // tm,),
        in_specs=[pl.BlockSpec((tm, 1), lambda i: (i, 0)), row, row],
        out_specs=[out, out],
        out_shape=[jax.ShapeDtypeStruct((T, RET_HD), F32)] * 2,
        compiler_params=_params("parallel"),
        name="rope_tables",
    )(positions.reshape(T, 1), inv2, sign)


def _rope(t, cos2, sin2):
    return t * cos2 + pltpu.roll(t, RET_HD // 2, 1) * sin2


def _ret_bwd_kernel(k_ref, v_ref, cos_ref, sin_ref, dv_ref, dc_ref, sb_ref, state, *, heads):
    @pl.when(pl.program_id(1) == 0)
    def _():
        state[...] = jnp.zeros_like(state)

    cos2, sin2 = cos_ref[...], sin_ref[...]
    for h in range(heads):
        sl = slice(h * RET_HD, (h + 1) * RET_HD)
        k = _rope(k_ref[:, sl], cos2, sin2)
        s = state[h]
        sb_ref[0, 0, h] = s.astype(BF16)
        state[h] = dc_ref[h] * s + _dot_tn(k * dv_ref[h, 1], v_ref[:, sl])


def _ret_fwd_kernel(q_ref, k_ref, v_ref, g_ref, cos_ref, sin_ref, dm_ref, dv_ref, dc_ref, sb_ref,
                    o_ref, state, *, heads):
    @pl.when(pl.program_id(1) == 0)
    def _():
        state[...] = jnp.zeros_like(state)

    cos2, sin2 = cos_ref[...], sin_ref[...]
    for h in range(heads):
        sl = slice(h * RET_HD, (h + 1) * RET_HD)
        q = _rope(q_ref[:, sl], cos2, sin2) * (RET_HD ** -0.5)
        k = _rope(k_ref[:, sl], cos2, sin2)
        v = v_ref[:, sl]
        s = state[h]
        y = _dot(_dot_nt(q, k) * dm_ref[h], v)
        y = y + _dot(q * dv_ref[h, 2], s)
        y = y + jnp.dot((q * dv_ref[h, 3]).astype(BF16), sb_ref[0, 0, h], preferred_element_type=F32)
        state[h] = dc_ref[h] * s + _dot_tn(k * dv_ref[h, 0], v)
        y = y * lax.rsqrt(jnp.mean(y * y, axis=-1, keepdims=True) + 1e-6)
        g = g_ref[:, sl]
        o_ref[:, sl] = (y * (g * jax.nn.sigmoid(g))).astype(BF16)


def _retention(z_ret, cos2, sin2, batch):
    T, W4 = z_ret.shape
    W = W4 // 4
    H, C, Dh = W // RET_HD, RET_CHUNK, RET_HD
    nC = T // batch // C
    lg = jnp.log(1.0 - 2.0 ** (-5.0 - jnp.arange(H, dtype=F32)))
    pos = jnp.arange(C, dtype=F32)
    dmat = jnp.exp(lg[:, None, None] * jnp.abs(pos[:, None] - pos[None, :]))
    dvec = jnp.stack([jnp.exp(lg[:, None] * (C - 1 - pos)), jnp.exp(lg[:, None] * pos),
                      jnp.exp(lg[:, None] * (pos + 1.0)), jnp.exp(lg[:, None] * (C - pos))], axis=1)
    dvec = jnp.broadcast_to(dvec[..., None], (H, 4, C, Dh))
    dcay = jnp.broadcast_to(jnp.exp(lg * C)[:, None, None], (H, 1, Dh))

    def rows(f):
        return lambda b, n: (b * nC + f(n), 0)

    def cols(f, j):
        return lambda b, n: (b * nC + f(n), j)

    rev = lambda n: nC - 1 - n
    same = lambda n: n
    const3 = lambda b, n: (0, 0, 0)
    const4 = lambda b, n: (0, 0, 0, 0)
    sb = pl.pallas_call(
        functools.partial(_ret_bwd_kernel, heads=H),
        grid=(batch, nC),
        in_specs=[pl.BlockSpec((C, W), cols(rev, 1)), pl.BlockSpec((C, W), cols(rev, 2)),
                  pl.BlockSpec((C, Dh), rows(rev)), pl.BlockSpec((C, Dh), rows(rev)),
                  pl.BlockSpec((H, 4, C, Dh), const4), pl.BlockSpec((H, 1, Dh), const3)],
        out_specs=pl.BlockSpec((1, 1, H, Dh, Dh), lambda b, n: (b, nC - 1 - n, 0, 0, 0)),
        out_shape=jax.ShapeDtypeStruct((batch, nC, H, Dh, Dh), BF16),
        scratch_shapes=[pltpu.VMEM((H, Dh, Dh), F32)],
        compiler_params=_params("parallel", "arbitrary"),
        name="ret_bwd",
    )(z_ret, z_ret, cos2, sin2, dvec, dcay)
    return pl.pallas_call(
        functools.partial(_ret_fwd_kernel, heads=H),
        grid=(batch, nC),
        in_specs=[pl.BlockSpec((C, W), cols(same, 0)), pl.BlockSpec((C, W), cols(same, 1)),
                  pl.BlockSpec((C, W), cols(same, 2)), pl.BlockSpec((C, W), cols(same, 3)),
                  pl.BlockSpec((C, Dh), rows(same)), pl.BlockSpec((C, Dh), rows(same)),
                  pl.BlockSpec((H, C, C), const3), pl.BlockSpec((H, 4, C, Dh), const4),
                  pl.BlockSpec((H, 1, Dh), const3),
                  pl.BlockSpec((1, 1, H, Dh, Dh), lambda b, n: (b, n, 0, 0, 0))],
        out_specs=pl.BlockSpec((C, W), rows(same)),
        out_shape=jax.ShapeDtypeStruct((T, W), BF16),
        scratch_shapes=[pltpu.VMEM((H, Dh, Dh), F32)],
        compiler_params=_params("parallel", "arbitrary"),
        name="ret_fwd",
    )(z_ret, z_ret, z_ret, z_ret, cos2, sin2, dmat, dvec, dcay, sb)


def _head_sum(x, e_ref):
    n = e_ref.shape[0]
    parts = [_dot32(x[:, j:j + n], e_ref[...]) for j in range(0, x.shape[1], n)]
    return jnp.concatenate(parts, axis=1)


def _rwkv_pre_kernel(z_ref, zp_ref, zn_ref, mup_ref, mun_ref, w0_ref, wup_ref, a0_ref, aup_ref, gup_ref,
                     kk_w_ref, ka_ref, rk_ref, e_ref,
                     r_out, v_out, kk_out, kd_out, bd_out, lw_out, g_out, bonus_out, *, tiles_per_seq, W):
    i = pl.program_id(0)
    z = z_ref[...]
    tm = z.shape[0]
    row = lax.broadcasted_iota(jnp.int32, z.shape, 0)
    first = (i % tiles_per_seq) == 0
    last = (i % tiles_per_seq) == tiles_per_seq - 1
    prev_row = jnp.where(first, 0.0, zp_ref[7:8, :])
    next_row = jnp.where(last, 0.0, zn_ref[0:1, :])
    z_prev = jnp.where(row == 0, prev_row, pltpu.roll(z, 1, 0))
    z_next = jnp.where(row == tm - 1, next_row, pltpu.roll(z, tm - 1, 0))
    zf = z + mup_ref[...] * (z_prev - z) + mun_ref[...] * (z_next - z)
    r, k, v = zf[:, 0:W], zf[:, W:2 * W], zf[:, 2 * W:3 * W]
    o = 3 * W
    wd = jnp.tanh(zf[:, o:o + 2 * DECAY_LORA])
    ad = zf[:, o + 2 * DECAY_LORA:o + 2 * DECAY_LORA + 2 * AAA_LORA]
    gd = jax.nn.sigmoid(zf[:, o + 2 * DECAY_LORA + 2 * AAA_LORA:])
    kk = k * kk_w_ref[...]
    kk = kk / jnp.maximum(jnp.sqrt(_head_sum(kk * kk, e_ref)), 1e-12)
    r_out[...] = r
    v_out[...] = v
    kk_out[...] = kk
    g_out[...] = _dot(gd, gup_ref[...])
    bonus_out[...] = _head_sum(r * k * rk_ref[...], e_ref) * v
    for d in range(2):
        w_raw = w0_ref[d] + _dot(wd, wup_ref[d])
        lw_out[d] = (-math.exp(-0.5)) * jax.nn.sigmoid(w_raw)
        a = jax.nn.sigmoid(a0_ref[d] + _dot(ad, aup_ref[d]))
        kd_out[d] = k * (1.0 + (a - 1.0) * ka_ref[...])
        bd_out[d] = kk * a


def _split_dot(m, x):
    mb = m.astype(BF16)
    x1 = x.astype(BF16)
    r1 = x - x1.astype(F32)
    x2 = r1.astype(BF16)
    x3 = (r1 - x2.astype(F32)).astype(BF16)
    dot = lambda t: jnp.dot(mb, t, preferred_element_type=F32)
    return dot(x1) + dot(x2) + dot(x3)


def _rwkv_chunk(direction, sl, r_ref, v_ref, kk_ref, kd_ref, bd_ref, lw, cs, ctot, y_ref, state, *, L, hpg):
    gw = hpg * L
    sgn = 1 if direction == 0 else -1
    lane = lax.broadcasted_iota(jnp.int32, (L, gw), 1)
    rowc = lax.broadcasted_iota(jnp.int32, (L, gw), 0) * sgn
    colc = jnp.bitwise_and(lane, L - 1) * sgn
    strict = colc < rowc
    incl = colc <= rowc
    eye_cat = jnp.where(colc == rowc, 1.0, 0.0).astype(F32)
    shift = int(math.log2(L))
    head_of_lane = jnp.right_shift(lane, shift)
    r2 = jnp.right_shift(lax.broadcasted_iota(jnp.int32, (gw, gw), 0), shift)
    c2 = jnp.right_shift(lax.broadcasted_iota(jnp.int32, (gw, gw), 1), shift)
    same_head = r2 == c2

    def bd(x):
        return jnp.concatenate([jnp.where(head_of_lane == h, x, 0.0) for h in range(hpg)], axis=0).astype(BF16)

    c_in = cs[:, sl]
    c_ex = c_in - lw[:, sl]
    einv = jnp.exp(-c_in)
    wend = jnp.exp(ctot[:, sl] - c_in)
    b, kd, vp = bd_ref[0, :, sl], kd_ref[0, :, sl], v_ref[:, sl]
    kq = kk_ref[:, sl] * jnp.exp(c_ex)
    rq = r_ref[:, sl] * jnp.exp(c_in)
    lhs = jnp.concatenate([kq, rq], axis=0).astype(BF16)
    g = _dot_nt(lhs, jnp.concatenate([bd(b * einv), bd(kd * einv)], axis=0))
    s = state[...]
    kr = _dot_nt(lhs, s)
    yield
    a_ab = jnp.where(strict, g[:L, :gw], 0.0)
    a_ak = jnp.where(strict, g[:L, gw:], 0.0)
    a_rb = jnp.where(incl, g[L:, :gw], 0.0)
    a_rk = jnp.where(incl, g[L:, gw:], 0.0)
    bvp = bd(vp)
    rhs = kr[:L] + _dot(a_ak, bvp)
    inv = eye_cat - a_ab
    apow = a_ab
    for _ in range(shift - 1):
        apow = _dot(apow, bd(apow))
        yield
        inv = inv + _dot(inv, bd(apow))
    yield
    u = _dot(inv, bd(-rhs))
    yield
    y = kr[L:] + _dot(jnp.concatenate([a_rb, a_rk], axis=1), jnp.concatenate([bd(u), bvp], axis=0))
    y_ref[:, sl] = y
    upd = _dot_tn(jnp.concatenate([u, vp], axis=0), jnp.concatenate([b * wend, kd * wend], axis=0))
    state[...] = s * jnp.exp(ctot[:, sl]) + jnp.where(same_head, upd, 0.0)


def _rwkv_scan_kernel(rf_ref, vf_ref, kkf_ref, rb_ref, vb_ref, kkb_ref, kdf_ref, bdf_ref, lwf_ref,
                      kdb_ref, bdb_ref, lwb_ref, yf_ref, yb_ref, state, *, L, hpg, ngroups):
    @pl.when(pl.program_id(1) == 0)
    def _():
        state[...] = jnp.zeros_like(state)

    row = lax.broadcasted_iota(jnp.int32, (L, L), 0)
    col = lax.broadcasted_iota(jnp.int32, (L, L), 1)
    gw = hpg * L
    chains = []
    for direction, (r_ref, v_ref, kk_ref, kd_ref, bd_ref, lw_ref, y_ref) in enumerate(
            [(rf_ref, vf_ref, kkf_ref, kdf_ref, bdf_ref, lwf_ref, yf_ref),
             (rb_ref, vb_ref, kkb_ref, kdb_ref, bdb_ref, lwb_ref, yb_ref)]):
        tri = jnp.where(col <= row if direction == 0 else col >= row, 1.0, 0.0).astype(F32)
        lw = lw_ref[0]
        cs = _split_dot(tri, lw)
        ctot = jnp.sum(lw, axis=0, keepdims=True)
        for gi in range(ngroups):
            chains.append(_rwkv_chunk(direction, slice(gi * gw, (gi + 1) * gw), r_ref, v_ref, kk_ref, kd_ref,
                                      bd_ref, lw, cs, ctot, y_ref, state.at[direction, gi], L=L, hpg=hpg))
    while chains:
        alive = []
        for chain in chains:
            if next(chain, "done") != "done":
                alive.append(chain)
        chains = alive


def _rwkv_post_kernel(yf_ref, yb_ref, g_ref, bonus_ref, lw_ref, lb_ref, e_ref, o_ref):
    y = yf_ref[...] + yb_ref[...]
    mu = _head_sum(y, e_ref) * (1.0 / RWKV_HD)
    dlt = y - mu
    var = _head_sum(dlt * dlt, e_ref) * (1.0 / RWKV_HD)
    yn = dlt * lax.rsqrt(var + RWKV_LNX_EPS) * lw_ref[...] + lb_ref[...]
    o_ref[...] = ((yn + bonus_ref[...]) * g_ref[...]).astype(BF16)


def _rwkv7(z, batch, mu_prev, mu_next, w0, w_up, a0, a_up, g_up, k_k, k_a, r_k, lnx_w, lnx_b):
    T, cols = z.shape
    W = w0.shape[-1]
    S = T // batch
    L = RWKV_CHUNK
    nC = S // L
    row = lambda x: x.reshape(1, -1).astype(F32)
    wup = jnp.zeros((2, 2 * DECAY_LORA, W), F32)
    wup = wup.at[0, :DECAY_LORA].set(w_up[0]).at[1, DECAY_LORA:].set(w_up[1]).astype(BF16)
    aup = jnp.zeros((2, 2 * AAA_LORA, W), F32)
    aup = aup.at[0, :AAA_LORA].set(a_up[0]).at[1, AAA_LORA:].set(a_up[1]).astype(BF16)
    eb = 4 * RWKV_HD
    hid = jnp.arange(eb) // RWKV_HD
    e = (hid[:, None] == hid[None, :]).astype(F32)
    tm = min(256, S)
    nt = T // tm
    full = lambda shape: pl.BlockSpec(shape, lambda i: (0,) * len(shape))
    tile = pl.BlockSpec((tm, W), lambda i: (i, 0))
    tile2 = pl.BlockSpec((2, tm, W), lambda i: (0, i, 0))
    hb = tm // 8
    r, v, kk, kd, bd, lw, g, bonus = pl.pallas_call(
        functools.partial(_rwkv_pre_kernel, tiles_per_seq=S // tm, W=W),
        grid=(nt,),
        in_specs=[pl.BlockSpec((tm, cols), lambda i: (i, 0)),
                  pl.BlockSpec((8, cols), lambda i: (jnp.maximum(i * hb - 1, 0), 0)),
                  pl.BlockSpec((8, cols), lambda i: (jnp.minimum((i + 1) * hb, T // 8 - 1), 0)),
                  full((1, cols)), full((1, cols)), full((2, 1, W)), full((2, 2 * DECAY_LORA, W)),
                  full((2, 1, W)), full((2, 2 * AAA_LORA, W)), full((GATE_LORA, W)),
                  full((1, W)), full((1, W)), full((1, W)), full((eb, eb))],
        out_specs=[tile, tile, tile, tile2, tile2, tile2, tile, tile],
        out_shape=[jax.ShapeDtypeStruct((T, W), F32)] * 3 + [jax.ShapeDtypeStruct((2, T, W), F32)] * 3
        + [jax.ShapeDtypeStruct((T, W), F32)] * 2,
        compiler_params=_params("parallel"),
        name="rwkv_pre",
    )(z, z, z, row(mu_prev), row(mu_next), w0.reshape(2, 1, W), wup, a0.reshape(2, 1, W), aup,
      g_up.astype(BF16), row(k_k), row(k_a), row(r_k), e)

    hpg = RWKV_HEADS_PER_GROUP
    gw = hpg * L
    fw = lambda b, c: b * nC + c
    bw = lambda b, c: b * nC + nC - 1 - c
    seq = lambda f: pl.BlockSpec((L, W), lambda b, c: (f(b, c), 0))
    seq2 = lambda d, f: pl.BlockSpec((1, L, W), lambda b, c: (d, f(b, c), 0))
    y_f, y_b = pl.pallas_call(
        functools.partial(_rwkv_scan_kernel, L=L, hpg=hpg, ngroups=W // gw),
        grid=(batch, nC),
        in_specs=[seq(fw)] * 3 + [seq(bw)] * 3 + [seq2(0, fw)] * 3 + [seq2(1, bw)] * 3,
        out_specs=[seq(fw), seq(bw)],
        out_shape=[jax.ShapeDtypeStruct((T, W), F32)] * 2,
        scratch_shapes=[pltpu.VMEM((2, W // gw, gw, gw), F32)],
        compiler_params=_params("parallel", "arbitrary"),
        name="rwkv_scan",
    )(r, v, kk, r, v, kk, kd, bd, lw, kd, bd, lw)

    return pl.pallas_call(
        _rwkv_post_kernel,
        grid=(nt,),
        in_specs=[tile, tile, tile, tile, full((1, W)), full((1, W)), full((eb, eb))],
        out_specs=tile,
        out_shape=jax.ShapeDtypeStruct((T, W), BF16),
        compiler_params=_params("parallel"),
        name="rwkv_post",
    )(y_f, y_b, g, bonus, row(lnx_w), row(lnx_b), e)


def _s5_scan_kernel(uf_ref, ub_ref, bre_ref, bim_ref, cre_ref, cim_ref, lre_ref, lim_ref, yf_ref, yb_ref,
                    sre, sim, st_re, st_im, *, tt, batch, nsb):
    @pl.when(pl.program_id(0) == 0)
    def _():
        st_re[...] = jnp.zeros_like(st_re)
        st_im[...] = jnp.zeros_like(st_im)

    rows = sre.shape[-2]
    gp = rows * LANES
    cw = uf_ref.shape[-1] // nsb
    sw = gp // nsb
    dirs = ((0, uf_ref, yf_ref), (1, ub_ref, yb_ref))
    for d, u_ref, _ in dirs:
        for b in range(batch):
            u = u_ref[b].astype(BF16)
            proj = lambda w_ref: jnp.concatenate(
                [jnp.dot(u[:, q * cw:(q + 1) * cw], w_ref[d, q], preferred_element_type=F32) for q in range(nsb)],
                axis=1).reshape(tt, rows, LANES)
            sre[d, b] = proj(bre_ref)
            sim[d, b] = proj(bim_ref)

    lam = [(lre_ref[d], lim_ref[d]) for d in range(2)]

    def body(s, carry):
        out = []
        for d in range(2):
            t = s if d == 0 else tt - 1 - s
            ar, ai = lam[d]
            for b in range(batch):
                xr, xi = carry[d * batch + b]
                nr = ar * xr - ai * xi + sre[d, b, t]
                ni = ar * xi + ai * xr + sim[d, b, t]
                sre[d, b, t] = nr
                sim[d, b, t] = ni
                out.append((nr, ni))
        return tuple(out)

    init = tuple((st_re[d, b], st_im[d, b]) for d in range(2) for b in range(batch))
    fin = lax.fori_loop(0, tt, body, init, unroll=8)
    for d in range(2):
        for b in range(batch):
            st_re[d, b], st_im[d, b] = fin[d * batch + b]

    for d, _, y_ref in dirs:
        for b in range(batch):
            xr = sre[d, b].reshape(tt, gp).astype(BF16)
            xi = sim[d, b].reshape(tt, gp).astype(BF16)
            y_ref[b] = jnp.concatenate(
                [jnp.dot(xr[:, q * sw:(q + 1) * sw], cre_ref[d, q], preferred_element_type=F32)
                 - jnp.dot(xi[:, q * sw:(q + 1) * sw], cim_ref[d, q], preferred_element_type=F32)
                 for q in range(nsb)], axis=1)


def _s5_out_kernel(yf_ref, yb_ref, u_ref, dsk_ref, gw_ref, gb_ref, o_ref):
    y = u_ref[...] * dsk_ref[...] + yf_ref[...] + yb_ref[...]
    zg = jax.nn.gelu(y)
    o_ref[...] = (zg * jax.nn.sigmoid(_dot(zg, gw_ref[...]) + gb_ref[...])).astype(BF16)


def _s5(u, batch, lam_re, lam_im, log_dt, b_re, b_im, c_re, c_im, d_skip, glu_w, glu_b):
    T, W = u.shape
    S = T // batch
    G, P, Hh = lam_re.shape[1], lam_re.shape[2], S5_GROUP
    GP = G * P
    dt = jnp.exp(log_dt)[..., None]
    mag = jnp.exp(lam_re * dt)
    ab_re, ab_im = mag * jnp.cos(lam_im * dt), mag * jnp.sin(lam_im * dt)
    den = lam_re * lam_re + lam_im * lam_im
    nr, ni = ab_re - 1.0, ab_im
    cr = (nr * lam_re + ni * lam_im) / den
    ci = (ni * lam_re - nr * lam_im) / den
    bb_re = cr[..., None] * b_re - ci[..., None] * b_im
    bb_im = cr[..., None] * b_im + ci[..., None] * b_re
    nsb = G // S5_SUPER
    eye = jnp.eye(S5_SUPER, dtype=F32)
    sb = lambda m: m.reshape(2, nsb, S5_SUPER, *m.shape[2:])
    to_in = lambda m: (eye[None, None, :, None, :, None] * sb(m).transpose(0, 1, 2, 4, 3)[:, :, :, :, None, :]
                       ).reshape(2, nsb, S5_SUPER * Hh, S5_SUPER * P).astype(BF16)
    to_out = lambda m: (eye[None, None, :, None, :, None] * sb(m).transpose(0, 1, 2, 4, 3)[:, :, :, :, None, :]
                        ).reshape(2, nsb, S5_SUPER * P, S5_SUPER * Hh).astype(BF16)
    bmat_re, bmat_im = to_in(bb_re), to_in(bb_im)
    cmat_re, cmat_im = to_out(c_re), to_out(c_im)
    rows = GP // LANES
    lam_r = ab_re.reshape(2, rows, LANES)
    lam_i = ab_im.reshape(2, rows, LANES)

    tt = min(128, S)
    nT = S // tt
    u3 = u.reshape(batch, S, W)
    fwd = pl.BlockSpec((batch, tt, W), lambda i: (0, i, 0))
    bwd = pl.BlockSpec((batch, tt, W), lambda i: (0, nT - 1 - i, 0))
    full = lambda shape: pl.BlockSpec(shape, lambda i: (0,) * len(shape))
    y_f, y_b = pl.pallas_call(
        functools.partial(_s5_scan_kernel, tt=tt, batch=batch, nsb=nsb),
        grid=(nT,),
        in_specs=[fwd, bwd, full(bmat_re.shape), full(bmat_im.shape), full(cmat_re.shape), full(cmat_im.shape),
                  full(lam_r.shape), full(lam_i.shape)],
        out_specs=[fwd, bwd],
        out_shape=[jax.ShapeDtypeStruct((batch, S, W), F32)] * 2,
        scratch_shapes=[pltpu.VMEM((2, batch, tt, rows, LANES), F32)] * 2
        + [pltpu.VMEM((2, batch, rows, LANES), F32)] * 2,
        compiler_params=_params("arbitrary"),
        name="s5_scan",
    )(u3, u3, bmat_re, bmat_im, cmat_re, cmat_im, lam_r, lam_i)

    tm = min(512, T)
    tile = pl.BlockSpec((tm, W), lambda i: (i, 0))
    return pl.pallas_call(
        _s5_out_kernel,
        grid=(T // tm,),
        in_specs=[tile, tile, tile, full((1, W)), full((W, W)), full((1, W))],
        out_specs=tile,
        out_shape=jax.ShapeDtypeStruct((T, W), BF16),
        compiler_params=_params("parallel"),
        name="s5_out",
    )(y_f.reshape(T, W), y_b.reshape(T, W), u, d_skip.reshape(1, W), glu_w.astype(BF16), glu_b.reshape(1, W))


def _outproj_ln_kernel(x_ref, a_ref, b_ref, c_ref, wa_ref, wb_ref, wc_ref, lw_ref, lb_ref, rw_ref, rb_ref,
                       o_ref, ob_ref, gate_ref, *, alpha):
    mix = (jnp.dot(a_ref[...], wa_ref[...], preferred_element_type=F32)
           + jnp.dot(b_ref[...], wb_ref[...], preferred_element_type=F32)
           + jnp.dot(c_ref[...], wc_ref[...], preferred_element_type=F32))
    y = _layernorm(alpha * x_ref[...] + mix, lw_ref[...], lb_ref[...])
    o_ref[...] = y
    ob_ref[...] = y.astype(BF16)
    gate_ref[...] = _route(y, rw_ref, rb_ref)


def _outproj_ln(x, ya, yb, yc, w_out_b, ln_w, ln_b, router_w, router_b, alpha):
    T, D = x.shape
    wa, wb, wc = ya.shape[1], yb.shape[1], yc.shape[1]
    assert wa == wb and (wa + wb) % wc == 0
    tm = min(256, T)
    rowt = lambda w: pl.BlockSpec((tm, w), lambda i: (i, 0))
    full = lambda shape: pl.BlockSpec(shape, lambda i: (0,) * len(shape))
    return pl.pallas_call(
        functools.partial(_outproj_ln_kernel, alpha=alpha),
        grid=(T // tm,),
        in_specs=[rowt(D), rowt(wa), rowt(wb), rowt(wc),
                  pl.BlockSpec((wa, D), lambda i: (0, 0)), pl.BlockSpec((wb, D), lambda i: (1, 0)),
                  pl.BlockSpec((wc, D), lambda i: ((wa + wb) // wc, 0)), full((1, D)), full((1, D)),
                  full((D, LANES)), full((1, LANES))],
        out_specs=[rowt(D), rowt(D), rowt(LANES)],
        out_shape=[jax.ShapeDtypeStruct((T, D), F32), jax.ShapeDtypeStruct((T, D), BF16),
                   jax.ShapeDtypeStruct((T, LANES), F32)],
        compiler_params=_params("parallel"),
        name="outproj_ln",
    )(x, ya, yb, yc, w_out_b, w_out_b, w_out_b, ln_w.reshape(1, D), ln_b.reshape(1, D), router_w, router_b)


def _first_max(vals):
    best = vals[0]
    for v in vals[1:]:
        best = jnp.maximum(best, v)
    idx = jnp.full(best.shape, len(vals) - 1, jnp.int32)
    for j in range(len(vals) - 2, -1, -1):
        idx = jnp.where(vals[j] == best, j, idx)
    return best, idx


def _route(x, w_ref, b_ref):
    logits = _dot32(x, w_ref[...]) + b_ref[...]
    ng, ne = N_GROUPS, EXPERTS_PER_GROUP
    gl = [logits[:, j:j + 1] for j in range(ng)]
    gmax, _ = _first_max(gl)
    ex = [jnp.exp(v - gmax) for v in gl]
    den = ex[0]
    for v in ex[1:]:
        den = den + v
    prob = [v / den for v in ex]
    p_top, grp = _first_max(prob)
    el = []
    for e in range(ne):
        acc = jnp.zeros_like(p_top)
        for g in range(ng):
            acc = jnp.where(grp == g, logits[:, ng + g * ne + e:ng + g * ne + e + 1], acc)
        el.append(acc)
    v1, i1 = _first_max(el)
    el2 = [jnp.where(i1 == e, -jnp.inf, el[e]) for e in range(ne)]
    v2, i2 = _first_max(el2)
    e2 = jnp.exp(v2 - v1)
    w1 = p_top * (1.0 / (1.0 + e2))
    w2 = p_top * (e2 / (1.0 + e2))
    lane = lax.broadcasted_iota(jnp.int32, logits.shape, 1)
    return (jnp.where(lane == grp * ne + i1, w1, 0.0) + jnp.where(lane == grp * ne + i2, w2, 0.0)
            + jnp.where(lane == ng * ne, grp.astype(F32), 0.0))


def _router_weights(router_g, router_g_b, router_e, router_e_b):
    D = router_g.shape[0]
    ng, ne = N_GROUPS, EXPERTS_PER_GROUP
    w = jnp.concatenate([router_g, router_e.transpose(1, 0, 2).reshape(D, ng * ne)], axis=1)
    w = jnp.pad(w, ((0, 0), (0, LANES - w.shape[1])))
    b = jnp.concatenate([router_g_b, router_e_b.reshape(ng * ne)])
    return w, jnp.pad(b, (0, LANES - b.shape[0])).reshape(1, LANES)


def _expert_mlp(xin, w1_ref, w3_ref, w2_ref):
    h1 = jnp.dot(xin, w1_ref[0], preferred_element_type=F32)
    h3 = jnp.dot(xin, w3_ref[0], preferred_element_type=F32)
    return _dot((h1 * jax.nn.sigmoid(h1)) * h3, w2_ref[0])


def _lane_pick(vals, lane, j):
    return jnp.sum(jnp.where(lane == j, vals, 0.0), axis=1, keepdims=True)


def _moe_ln_kernel(x_ref, xb_ref, gate_ref, tri_ref, w1_ref, w3_ref, w2_ref, lw_ref, lb_ref, o_ref, ob_ref,
                   acc, rank, pt, xg, yg, gg, fits, *, alpha, n_experts, epg, cap):
    e = pl.program_id(1)
    g_idx = e // epg
    k = e % epg
    gate = gate_ref[...]
    lane = lax.broadcasted_iota(jnp.int32, gate.shape, 1)
    grp = _lane_pick(gate, lane, n_experts)

    @pl.when(e == 0)
    def _():
        acc[...] = jnp.zeros_like(acc)
        onehot = jnp.where(jnp.logical_and(lane < n_experts // epg, lane.astype(F32) == grp), 1.0, 0.0)
        rank[...] = jnp.dot(tri_ref[...], onehot.astype(BF16), preferred_element_type=F32)
        fits[0] = jnp.where(jnp.max(jnp.sum(onehot, axis=0, keepdims=True)) <= cap, 1, 0)

    sparse = fits[0] == 1

    @pl.when(jnp.logical_and(sparse, k == 0))
    def _():
        mine = grp == g_idx.astype(F32)
        slot = _lane_pick(rank[...], lane, g_idx)
        col = lax.broadcasted_iota(jnp.int32, pt.shape, 1).astype(F32)
        p = jnp.where(jnp.logical_and(mine, slot == col), 1.0, 0.0).astype(BF16)
        pt[...] = p
        xg[...] = _dot_tn(p, xb_ref[...]).astype(BF16)
        g1 = gate.astype(BF16)
        r1 = gate - g1.astype(F32)
        g2 = r1.astype(BF16)
        g3 = (r1 - g2.astype(F32)).astype(BF16)
        gg[...] = _dot_tn(p, g1) + _dot_tn(p, g2) + _dot_tn(p, g3)
        yg[...] = jnp.zeros_like(yg)

    @pl.when(sparse)
    def _():
        lane_c = lax.broadcasted_iota(jnp.int32, gg.shape, 1)
        yg[...] += _lane_pick(gg[...], lane_c, e) * _expert_mlp(xg[...], w1_ref, w3_ref, w2_ref)

    @pl.when(jnp.logical_and(sparse, k == epg - 1))
    def _():
        y = yg[...]
        y1 = y.astype(BF16)
        y2 = (y - y1.astype(F32)).astype(BF16)
        p = pt[...]
        acc[...] += jnp.dot(p, y1, preferred_element_type=F32) + jnp.dot(p, y2, preferred_element_type=F32)

    @pl.when(jnp.logical_not(sparse))
    def _():
        acc[...] += _lane_pick(gate, lane, e) * _expert_mlp(xb_ref[...], w1_ref, w3_ref, w2_ref)

    @pl.when(e == n_experts - 1)
    def _():
        y = _layernorm(alpha * x_ref[...] + acc[...], lw_ref[...], lb_ref[...])
        o_ref[...] = y
        ob_ref[...] = y.astype(BF16)


def _moe_ln(x, xb, gate, w1b, w3b, w2b, ln_w, ln_b, alpha):
    T, D = x.shape
    E, _, F = w1b.shape
    tm = min(512, T)
    cap = MOE_CAP_NUM * tm // MOE_CAP_DEN
    tri = (jnp.arange(tm)[None, :] < jnp.arange(tm)[:, None]).astype(BF16)
    rowt = lambda w: pl.BlockSpec((tm, w), lambda i, e: (i, 0))
    full = lambda shape: pl.BlockSpec(shape, lambda i, e: (0,) * len(shape))
    return pl.pallas_call(
        functools.partial(_moe_ln_kernel, alpha=alpha, n_experts=E, epg=EXPERTS_PER_GROUP, cap=cap),
        grid=(T // tm, E),
        in_specs=[rowt(D), rowt(D), rowt(LANES), full((tm, tm)),
                  pl.BlockSpec((1, D, F), lambda i, e: (e, 0, 0)), pl.BlockSpec((1, D, F), lambda i, e: (e, 0, 0)),
                  pl.BlockSpec((1, F, D), lambda i, e: (e, 0, 0)), full((1, D)), full((1, D))],
        out_specs=[rowt(D), rowt(D)],
        out_shape=[jax.ShapeDtypeStruct((T, D), F32), jax.ShapeDtypeStruct((T, D), BF16)],
        scratch_shapes=[pltpu.VMEM((tm, D), F32), pltpu.VMEM((tm, LANES), F32), pltpu.VMEM((tm, cap), BF16),
                        pltpu.VMEM((cap, D), BF16), pltpu.VMEM((cap, D), F32), pltpu.VMEM((cap, LANES), F32),
                        pltpu.SMEM((1,), jnp.int32)],
        compiler_params=_params("parallel", "arbitrary"),
        name="moe_ln",
    )(x, xb, gate, tri, w1b, w3b, w2b, ln_w.reshape(1, D), ln_b.reshape(1, D))


def _ple_ln_kernel(x_ref, xb_ref, p_ref, wg_ref, wp_ref, lw_ref, lb_ref, o_ref, ob_ref, *, alpha):
    gate = jax.nn.sigmoid(jnp.dot(xb_ref[...], wg_ref[...], preferred_element_type=F32))
    ple = gate * _dot(p_ref[...], wp_ref[...])
    y = _layernorm(alpha * x_ref[...] + ple, lw_ref[...], lb_ref[...])
    o_ref[...] = y
    ob_ref[...] = y.astype(BF16)


def _ple_ln(x, xb, p, wgb, wpb, ln_w, ln_b, alpha):
    T, D = x.shape
    Pd = p.shape[1]
    tm = min(256, T)
    rowt = lambda w: pl.BlockSpec((tm, w), lambda i: (i, 0))
    full = lambda shape: pl.BlockSpec(shape, lambda i: (0,) * len(shape))
    return pl.pallas_call(
        functools.partial(_ple_ln_kernel, alpha=alpha),
        grid=(T // tm,),
        in_specs=[rowt(D), rowt(D), rowt(Pd), full((D, D)), full((Pd, D)), full((1, D)), full((1, D))],
        out_specs=[rowt(D), rowt(D)],
        out_shape=[jax.ShapeDtypeStruct((T, D), F32), jax.ShapeDtypeStruct((T, D), BF16)],
        compiler_params=_params("parallel"),
        name="ple_ln",
    )(x, xb, p, wgb, wpb, ln_w.reshape(1, D), ln_b.reshape(1, D))


def kernel(x, p, positions, w_in, w_out, rwkv_mu_prev, rwkv_mu_next, rwkv_w0, rwkv_w_up, rwkv_a0, rwkv_a_up, rwkv_g_up, rwkv_k_k, rwkv_k_a, rwkv_r_k, rwkv_lnx_w, rwkv_lnx_b, s5_lam_re, s5_lam_im, s5_log_dt, s5_b_re, s5_b_im, s5_c_re, s5_c_im, s5_d, s5_glu_w, s5_glu_b, moe_router_g, moe_router_g_b, moe_router_e, moe_router_e_b, moe_w1, moe_w3, moe_w2, ple_proj, ple_gate, ln_w, ln_b):
    B, S, D = x.shape
    depth = w_in.shape[0]
    T = B * S
    alpha = (2.0 * depth) ** 0.25
    ret_w = rwkv_w = rwkv_w0.shape[-1]
    s5_w = s5_d.shape[-1]
    ret_cols = 4 * ret_w
    rwkv_cols = rwkv_mu_prev.shape[-1]

    cos2, sin2 = _rope_tables(positions)
    xf = x.reshape(T, D)
    xb = xf.astype(BF16)
    for i in range(depth):
        wi = w_in[i].astype(BF16)
        z_ret = _matmul(xb, wi[:, :ret_cols], 1024, 1024)
        z_rwkv = _matmul(xb, wi[:, ret_cols:ret_cols + rwkv_cols], 1024, rwkv_cols // 3)
        z_s5 = _matmul(xb, wi[:, ret_cols + rwkv_cols:], 1024, s5_w)
        y_ret = _retention(z_ret, cos2, sin2, B)
        y_rwkv = _rwkv7(z_rwkv, B, rwkv_mu_prev[i], rwkv_mu_next[i], rwkv_w0[i], rwkv_w_up[i], rwkv_a0[i],
                        rwkv_a_up[i], rwkv_g_up[i], rwkv_k_k[i], rwkv_k_a[i], rwkv_r_k[i], rwkv_lnx_w[i],
                        rwkv_lnx_b[i])
        y_s5 = _s5(z_s5, B, s5_lam_re[i], s5_lam_im[i], s5_log_dt[i], s5_b_re[i], s5_b_im[i], s5_c_re[i],
                   s5_c_im[i], s5_d[i], s5_glu_w[i], s5_glu_b[i])
        rw, rb = _router_weights(moe_router_g[i], moe_router_g_b[i], moe_router_e[i], moe_router_e_b[i])
        xf, xb, gate = _outproj_ln(xf, y_ret, y_rwkv, y_s5, w_out[i].astype(BF16), ln_w[i, 0], ln_b[i, 0], rw, rb,
                                   alpha)
        xf, xb = _moe_ln(xf, xb, gate, moe_w1[i].astype(BF16), moe_w3[i].astype(BF16), moe_w2[i].astype(BF16),
                         ln_w[i, 1], ln_b[i, 1], alpha)
        xf, xb = _ple_ln(xf, xb, p[i].reshape(T, -1), ple_gate[i].astype(BF16), ple_proj[i].astype(BF16),
                         ln_w[i, 2], ln_b[i, 2], alpha)
    return xf.reshape(B, S, D)
```

```python
import functools
import math

import jax
import jax.numpy as jnp
import numpy as np
from jax import lax
from jax.experimental import pallas as pl
from jax.experimental.pallas import tpu as pltpu

F32 = jnp.float32
BF16 = jnp.bfloat16
HIGHEST = lax.Precision.HIGHEST

LANES = 128
VMEM_LIMIT = 56 * 1024 * 1024

RET_HD = 128
RET_CHUNK = 128
ROPE_BASE = 10000.0
RWKV_HD = 64
RWKV_CHUNK = 64
RWKV_HEADS_PER_GROUP = 4
DECAY_LORA = 64
AAA_LORA = 64
GATE_LORA = 128
S5_GROUP = 16
S5_STATE = 64
S5_SUPER = 8
N_GROUPS = 4
EXPERTS_PER_GROUP = 4
MOE_TILE = 1024
MOE_SUBTILE = 512
MOE_CAP_NUM, MOE_CAP_DEN = 5, 16
LN_EPS = 1e-5
RWKV_LNX_EPS = 64e-5


def _params(*sem):
    return pltpu.CompilerParams(dimension_semantics=sem, vmem_limit_bytes=VMEM_LIMIT)


def _dot(a, b):
    return jnp.dot(a.astype(BF16), b.astype(BF16), preferred_element_type=F32)


def _dot_nt(a, b):
    return lax.dot_general(a.astype(BF16), b.astype(BF16), (((1,), (1,)), ((), ())),
                           preferred_element_type=F32)


def _dot_tn(a, b):
    return lax.dot_general(a.astype(BF16), b.astype(BF16), (((0,), (0,)), ((), ())),
                           preferred_element_type=F32)


def _dot32(a, b):
    return jnp.dot(a, b, precision=HIGHEST, preferred_element_type=F32)


def _dot32_nt(a, b):
    return lax.dot_general(a, b, (((1,), (1,)), ((), ())), precision=HIGHEST,
                           preferred_element_type=F32)


def _dot32_tn(a, b):
    return lax.dot_general(a, b, (((0,), (0,)), ((), ())), precision=HIGHEST,
                           preferred_element_type=F32)


def _layernorm(h, w, b):
    mu = jnp.mean(h, axis=-1, keepdims=True)
    d = h - mu
    var = jnp.mean(d * d, axis=-1, keepdims=True)
    return d * lax.rsqrt(var + LN_EPS) * w + b


def _matmul_kernel(x_ref, w_ref, o_ref):
    o_ref[...] = jnp.dot(x_ref[...], w_ref[...], preferred_element_type=F32)


def _matmul(xb, wb, tm, tn):
    T, K = xb.shape
    N = wb.shape[1]
    tm = min(tm, T)
    return pl.pallas_call(
        _matmul_kernel,
        grid=(N // tn, T // tm),
        in_specs=[pl.BlockSpec((tm, K), lambda j, i: (i, 0)),
                  pl.BlockSpec((K, tn), lambda j, i: (0, j))],
        out_specs=pl.BlockSpec((tm, tn), lambda j, i: (i, j)),
        out_shape=jax.ShapeDtypeStruct((T, N), F32),
        compiler_params=_params("parallel", "parallel"),
        name="in_proj",
    )(xb, wb)


def _rope_table_kernel(pos_ref, inv_ref, sign_ref, cos_ref, sin_ref):
    ang = pos_ref[...].astype(F32) * inv_ref[...]
    cos_ref[...] = jnp.cos(ang)
    sin_ref[...] = jnp.sin(ang) * sign_ref[...]


def _rope_tables(positions):
    T = positions.size
    half = RET_HD // 2
    inv = ROPE_BASE ** (-jnp.arange(half, dtype=F32) / half)
    inv2 = jnp.concatenate([inv, inv])[None, :]
    sign = jnp.concatenate([-jnp.ones((half,), F32), jnp.ones((half,), F32)])[None, :]
    tm = min(1024, T)
    row = pl.BlockSpec((1, RET_HD), lambda i: (0, 0))
    out = pl.BlockSpec((tm, RET_HD), lambda i: (i, 0))
    return pl.pallas_call(
        _rope_table_kernel,
        grid=(T // tm,),
        in_specs=[pl.BlockSpec((tm, 1), lambda i: (i, 0)), row, row],
        out_specs=[out, out],
        out_shape=[jax.ShapeDtypeStruct((T, RET_HD), F32)] * 2,
        compiler_params=_params("parallel"),
        name="rope_tables",
    )(positions.reshape(T, 1), inv2, sign)


def _rope(t, cos2, sin2):
    return t * cos2 + pltpu.roll(t, RET_HD // 2, 1) * sin2


def _ret_bwd_kernel(k_ref, v_ref, cos_ref, sin_ref, dv_ref, dc_ref, sb_ref, state, *, heads):
    @pl.when(pl.program_id(1) == 0)
    def _():
        state[...] = jnp.zeros_like(state)

    cos2, sin2 = cos_ref[...], sin_ref[...]
    for h in range(heads):
        sl = slice(h * RET_HD, (h + 1) * RET_HD)
        k = _rope(k_ref[:, sl], cos2, sin2)
        s = state[h]
        sb_ref[0, 0, h] = s.astype(BF16)
        state[h] = dc_ref[h] * s + _dot_tn(k * dv_ref[h, 1], v_ref[:, sl])


def _ret_fwd_kernel(q_ref, k_ref, v_ref, g_ref, cos_ref, sin_ref, dm_ref, dv_ref, dc_ref, sb_ref,
                    o_ref, state, *, heads):
    @pl.when(pl.program_id(1) == 0)
    def _():
        state[...] = jnp.zeros_like(state)

    cos2, sin2 = cos_ref[...], sin_ref[...]
    for h in range(heads):
        sl = slice(h * RET_HD, (h + 1) * RET_HD)
        q = _rope(q_ref[:, sl], cos2, sin2) * (RET_HD ** -0.5)
        k = _rope(k_ref[:, sl], cos2, sin2)
        v = v_ref[:, sl]
        s = state[h]
        y = _dot(_dot_nt(q, k) * dm_ref[h], v)
        y = y + _dot(q * dv_ref[h, 2], s)
        y = y + jnp.dot((q * dv_ref[h, 3]).astype(BF16), sb_ref[0, 0, h], preferred_element_type=F32)
        state[h] = dc_ref[h] * s + _dot_tn(k * dv_ref[h, 0], v)
        y = y * lax.rsqrt(jnp.mean(y * y, axis=-1, keepdims=True) + 1e-6)
        g = g_ref[:, sl]
        o_ref[:, sl] = (y * (g * jax.nn.sigmoid(g))).astype(BF16)


def _retention(z_ret, cos2, sin2, batch):
    T, W4 = z_ret.shape
    W = W4 // 4
    H, C, Dh = W // RET_HD, RET_CHUNK, RET_HD
    nC = T // batch // C
    lg = jnp.log(1.0 - 2.0 ** (-5.0 - jnp.arange(H, dtype=F32)))
    pos = jnp.arange(C, dtype=F32)
    dmat = jnp.exp(lg[:, None, None] * jnp.abs(pos[:, None] - pos[None, :]))
    dvec = jnp.stack([jnp.exp(lg[:, None] * (C - 1 - pos)), jnp.exp(lg[:, None] * pos),
                      jnp.exp(lg[:, None] * (pos + 1.0)), jnp.exp(lg[:, None] * (C - pos))], axis=1)
    dvec = jnp.broadcast_to(dvec[..., None], (H, 4, C, Dh))
    dcay = jnp.broadcast_to(jnp.exp(lg * C)[:, None, None], (H, 1, Dh))

    def rows(f):
        return lambda b, n: (b * nC + f(n), 0)

    def cols(f, j):
        return lambda b, n: (b * nC + f(n), j)

    rev = lambda n: nC - 1 - n
    same = lambda n: n
    const3 = lambda b, n: (0, 0, 0)
    const4 = lambda b, n: (0, 0, 0, 0)
    sb = pl.pallas_call(
        functools.partial(_ret_bwd_kernel, heads=H),
        grid=(batch, nC),
        in_specs=[pl.BlockSpec((C, W), cols(rev, 1)), pl.BlockSpec((C, W), cols(rev, 2)),
                  pl.BlockSpec((C, Dh), rows(rev)), pl.BlockSpec((C, Dh), rows(rev)),
                  pl.BlockSpec((H, 4, C, Dh), const4), pl.BlockSpec((H, 1, Dh), const3)],
        out_specs=pl.BlockSpec((1, 1, H, Dh, Dh), lambda b, n: (b, nC - 1 - n, 0, 0, 0)),
        out_shape=jax.ShapeDtypeStruct((batch, nC, H, Dh, Dh), BF16),
        scratch_shapes=[pltpu.VMEM((H, Dh, Dh), F32)],
        compiler_params=_params("parallel", "arbitrary"),
        name="ret_bwd",
    )(z_ret, z_ret, cos2, sin2, dvec, dcay)
    return pl.pallas_call(
        functools.partial(_ret_fwd_kernel, heads=H),
        grid=(batch, nC),
        in_specs=[pl.BlockSpec((C, W), cols(same, 0)), pl.BlockSpec((C, W), cols(same, 1)),
                  pl.BlockSpec((C, W), cols(same, 2)), pl.BlockSpec((C, W), cols(same, 3)),
                  pl.BlockSpec((C, Dh), rows(same)), pl.BlockSpec((C, Dh), rows(same)),
                  pl.BlockSpec((H, C, C), const3), pl.BlockSpec((H, 4, C, Dh), const4),
                  pl.BlockSpec((H, 1, Dh), const3),
                  pl.BlockSpec((1, 1, H, Dh, Dh), lambda b, n: (b, n, 0, 0, 0))],
        out_specs=pl.BlockSpec((C, W), rows(same)),
        out_shape=jax.ShapeDtypeStruct((T, W), BF16),
        scratch_shapes=[pltpu.VMEM((H, Dh, Dh), F32)],
        compiler_params=_params("parallel", "arbitrary"),
        name="ret_fwd",
    )(z_ret, z_ret, z_ret, z_ret, cos2, sin2, dmat, dvec, dcay, sb)


def _head_sum(x, e_ref):
    n = e_ref.shape[0]
    parts = [_dot32(x[:, j:j + n], e_ref[...]) for j in range(0, x.shape[1], n)]
    return jnp.concatenate(parts, axis=1)


def _rwkv_pre_kernel(z_ref, zp_ref, zn_ref, mup_ref, mun_ref, w0_ref, wup_ref, a0_ref, aup_ref, gup_ref,
                     kk_w_ref, ka_ref, rk_ref, e_ref,
                     r_out, v_out, kk_out, kd_out, bd_out, lw_out, g_out, bonus_out, *, tiles_per_seq, W):
    i = pl.program_id(0)
    z = z_ref[...]
    tm = z.shape[0]
    row = lax.broadcasted_iota(jnp.int32, z.shape, 0)
    first = (i % tiles_per_seq) == 0
    last = (i % tiles_per_seq) == tiles_per_seq - 1
    prev_row = jnp.where(first, 0.0, zp_ref[7:8, :])
    next_row = jnp.where(last, 0.0, zn_ref[0:1, :])
    z_prev = jnp.where(row == 0, prev_row, pltpu.roll(z, 1, 0))
    z_next = jnp.where(row == tm - 1, next_row, pltpu.roll(z, tm - 1, 0))
    zf = z + mup_ref[...] * (z_prev - z) + mun_ref[...] * (z_next - z)
    r, k, v = zf[:, 0:W], zf[:, W:2 * W], zf[:, 2 * W:3 * W]
    o = 3 * W
    wd = jnp.tanh(zf[:, o:o + 2 * DECAY_LORA])
    ad = zf[:, o + 2 * DECAY_LORA:o + 2 * DECAY_LORA + 2 * AAA_LORA]
    gd = jax.nn.sigmoid(zf[:, o + 2 * DECAY_LORA + 2 * AAA_LORA:])
    kk = k * kk_w_ref[...]
    kk = kk / jnp.maximum(jnp.sqrt(_head_sum(kk * kk, e_ref)), 1e-12)
    r_out[...] = r
    v_out[...] = v
    kk_out[...] = kk
    g_out[...] = _dot(gd, gup_ref[...])
    bonus_out[...] = _head_sum(r * k * rk_ref[...], e_ref) * v
    for d in range(2):
        w_raw = w0_ref[d] + _dot(wd, wup_ref[d])
        lw_out[d] = (-math.exp(-0.5)) * jax.nn.sigmoid(w_raw)
        a = jax.nn.sigmoid(a0_ref[d] + _dot(ad, aup_ref[d]))
        kd_out[d] = k * (1.0 + (a - 1.0) * ka_ref[...])
        bd_out[d] = kk * a


def _split_dot(m, x):
    mb = m.astype(BF16)
    x1 = x.astype(BF16)
    r1 = x - x1.astype(F32)
    x2 = r1.astype(BF16)
    x3 = (r1 - x2.astype(F32)).astype(BF16)
    dot = lambda t: jnp.dot(mb, t, preferred_element_type=F32)
    return dot(x1) + dot(x2) + dot(x3)


def _rwkv_chunk(direction, sl, r_ref, v_ref, kk_ref, kd_ref, bd_ref, lw, cs, ctot, y_ref, state, *, L, hpg):
    gw = hpg * L
    sgn = 1 if direction == 0 else -1
    lane = lax.broadcasted_iota(jnp.int32, (L, gw), 1)
    rowc = lax.broadcasted_iota(jnp.int32, (L, gw), 0) * sgn
    colc = jnp.bitwise_and(lane, L - 1) * sgn
    strict = colc < rowc
    incl = colc <= rowc
    eye_cat = jnp.where(colc == rowc, 1.0, 0.0).astype(F32)
    shift = int(math.log2(L))
    head_of_lane = jnp.right_shift(lane, shift)
    r2 = jnp.right_shift(lax.broadcasted_iota(jnp.int32, (gw, gw), 0), shift)
    c2 = jnp.right_shift(lax.broadcasted_iota(jnp.int32, (gw, gw), 1), shift)
    same_head = r2 == c2

    def bd(x):
        return jnp.concatenate([jnp.where(head_of_lane == h, x, 0.0) for h in range(hpg)], axis=0).astype(BF16)

    c_in = cs[:, sl]
    c_ex = c_in - lw[:, sl]
    einv = jnp.exp(-c_in)
    wend = jnp.exp(ctot[:, sl] - c_in)
    b, kd, vp = bd_ref[0, :, sl], kd_ref[0, :, sl], v_ref[:, sl]
    kq = kk_ref[:, sl] * jnp.exp(c_ex)
    rq = r_ref[:, sl] * jnp.exp(c_in)
    lhs = jnp.concatenate([kq, rq], axis=0).astype(BF16)
    g = _dot_nt(lhs, jnp.concatenate([bd(b * einv), bd(kd * einv)], axis=0))
    s = state[...]
    kr = _dot_nt(lhs, s)
    yield
    a_ab = jnp.where(strict, g[:L, :gw], 0.0)
    a_ak = jnp.where(strict, g[:L, gw:], 0.0)
    a_rb = jnp.where(incl, g[L:, :gw], 0.0)
    a_rk = jnp.where(incl, g[L:, gw:], 0.0)
    bvp = bd(vp)
    rhs = kr[:L] + _dot(a_ak, bvp)
    inv = eye_cat - a_ab
    apow = a_ab
    for _ in range(shift - 1):
        apow = _dot(apow, bd(apow))
        yield
        inv = inv + _dot(inv, bd(apow))
    yield
    u = _dot(inv, bd(-rhs))
    yield
    y = kr[L:] + _dot(jnp.concatenate([a_rb, a_rk], axis=1), jnp.concatenate([bd(u), bvp], axis=0))
    y_ref[:, sl] = y
    upd = _dot_tn(jnp.concatenate([u, vp], axis=0), jnp.concatenate([b * wend, kd * wend], axis=0))
    state[...] = s * jnp.exp(ctot[:, sl]) + jnp.where(same_head, upd, 0.0)


def _rwkv_scan_kernel(rf_ref, vf_ref, kkf_ref, rb_ref, vb_ref, kkb_ref, kdf_ref, bdf_ref, lwf_ref,
                      kdb_ref, bdb_ref, lwb_ref, yf_ref, yb_ref, state, *, L, hpg, ngroups):
    @pl.when(pl.program_id(1) == 0)
    def _():
        state[...] = jnp.zeros_like(state)

    row = lax.broadcasted_iota(jnp.int32, (L, L), 0)
    col = lax.broadcasted_iota(jnp.int32, (L, L), 1)
    gw = hpg * L
    chains = []
    for direction, (r_ref, v_ref, kk_ref, kd_ref, bd_ref, lw_ref, y_ref) in enumerate(
            [(rf_ref, vf_ref, kkf_ref, kdf_ref, bdf_ref, lwf_ref, yf_ref),
             (rb_ref, vb_ref, kkb_ref, kdb_ref, bdb_ref, lwb_ref, yb_ref)]):
        tri = jnp.where(col <= row if direction == 0 else col >= row, 1.0, 0.0).astype(F32)
        lw = lw_ref[0]
        cs = _split_dot(tri, lw)
        ctot = jnp.sum(lw, axis=0, keepdims=True)
        for gi in range(ngroups):
            chains.append(_rwkv_chunk(direction, slice(gi * gw, (gi + 1) * gw), r_ref, v_ref, kk_ref, kd_ref,
                                      bd_ref, lw, cs, ctot, y_ref, state.at[direction, gi], L=L, hpg=hpg))
    while chains:
        alive = []
        for chain in chains:
            if next(chain, "done") != "done":
                alive.append(chain)
        chains = alive


def _rwkv_post_kernel(yf_ref, yb_ref, g_ref, bonus_ref, lw_ref, lb_ref, e_ref, o_ref):
    y = yf_ref[...] + yb_ref[...]
    mu = _head_sum(y, e_ref) * (1.0 / RWKV_HD)
    dlt = y - mu
    var = _head_sum(dlt * dlt, e_ref) * (1.0 / RWKV_HD)
    yn = dlt * lax.rsqrt(var + RWKV_LNX_EPS) * lw_ref[...] + lb_ref[...]
    o_ref[...] = ((yn + bonus_ref[...]) * g_ref[...]).astype(BF16)


def _rwkv7(z, batch, mu_prev, mu_next, w0, w_up, a0, a_up, g_up, k_k, k_a, r_k, lnx_w, lnx_b):
    T, cols = z.shape
    W = w0.shape[-1]
    S = T // batch
    L = RWKV_CHUNK
    nC = S // L
    row = lambda x: x.reshape(1, -1).astype(F32)
    wup = jnp.zeros((2, 2 * DECAY_LORA, W), F32)
    wup = wup.at[0, :DECAY_LORA].set(w_up[0]).at[1, DECAY_LORA:].set(w_up[1]).astype(BF16)
    aup = jnp.zeros((2, 2 * AAA_LORA, W), F32)
    aup = aup.at[0, :AAA_LORA].set(a_up[0]).at[1, AAA_LORA:].set(a_up[1]).astype(BF16)
    eb = 4 * RWKV_HD
    hid = jnp.arange(eb) // RWKV_HD
    e = (hid[:, None] == hid[None, :]).astype(F32)
    tm = min(256, S)
    nt = T // tm
    full = lambda shape: pl.BlockSpec(shape, lambda i: (0,) * len(shape))
    tile = pl.BlockSpec((tm, W), lambda i: (i, 0))
    tile2 = pl.BlockSpec((2, tm, W), lambda i: (0, i, 0))
    hb = tm // 8
    r, v, kk, kd, bd, lw, g, bonus = pl.pallas_call(
        functools.partial(_rwkv_pre_kernel, tiles_per_seq=S // tm, W=W),
        grid=(nt,),
        in_specs=[pl.BlockSpec((tm, cols), lambda i: (i, 0)),
                  pl.BlockSpec((8, cols), lambda i: (jnp.maximum(i * hb - 1, 0), 0)),
                  pl.BlockSpec((8, cols), lambda i: (jnp.minimum((i + 1) * hb, T // 8 - 1), 0)),
                  full((1, cols)), full((1, cols)), full((2, 1, W)), full((2, 2 * DECAY_LORA, W)),
                  full((2, 1, W)), full((2, 2 * AAA_LORA, W)), full((GATE_LORA, W)),
                  full((1, W)), full((1, W)), full((1, W)), full((eb, eb))],
        out_specs=[tile, tile, tile, tile2, tile2, tile2, tile, tile],
        out_shape=[jax.ShapeDtypeStruct((T, W), F32)] * 3 + [jax.ShapeDtypeStruct((2, T, W), F32)] * 3
        + [jax.ShapeDtypeStruct((T, W), F32)] * 2,
        compiler_params=_params("parallel"),
        name="rwkv_pre",
    )(z, z, z, row(mu_prev), row(mu_next), w0.reshape(2, 1, W), wup, a0.reshape(2, 1, W), aup,
      g_up.astype(BF16), row(k_k), row(k_a), row(r_k), e)

    hpg = RWKV_HEADS_PER_GROUP
    gw = hpg * L
    fw = lambda b, c: b * nC + c
    bw = lambda b, c: b * nC + nC - 1 - c
    seq = lambda f: pl.BlockSpec((L, W), lambda b, c: (f(b, c), 0))
    seq2 = lambda d, f: pl.BlockSpec((1, L, W), lambda b, c: (d, f(b, c), 0))
    y_f, y_b = pl.pallas_call(
        functools.partial(_rwkv_scan_kernel, L=L, hpg=hpg, ngroups=W // gw),
        grid=(batch, nC),
        in_specs=[seq(fw)] * 3 + [seq(bw)] * 3 + [seq2(0, fw)] * 3 + [seq2(1, bw)] * 3,
        out_specs=[seq(fw), seq(bw)],
        out_shape=[jax.ShapeDtypeStruct((T, W), F32)] * 2,
        scratch_shapes=[pltpu.VMEM((2, W // gw, gw, gw), F32)],
        compiler_params=_params("parallel", "arbitrary"),
        name="rwkv_scan",
    )(r, v, kk, r, v, kk, kd, bd, lw, kd, bd, lw)

    return pl.pallas_call(
        _rwkv_post_kernel,
        grid=(nt,),
        in_specs=[tile, tile, tile, tile, full((1, W)), full((1, W)), full((eb, eb))],
        out_specs=tile,
        out_shape=jax.ShapeDtypeStruct((T, W), BF16),
        compiler_params=_params("parallel"),
        name="rwkv_post",
    )(y_f, y_b, g, bonus, row(lnx_w), row(lnx_b), e)


def _s5_scan_kernel(uf_ref, ub_ref, bre_ref, bim_ref, cre_ref, cim_ref, lre_ref, lim_ref, yf_ref, yb_ref,
                    sre, sim, st_re, st_im, *, tt, batch, nsb):
    @pl.when(pl.program_id(0) == 0)
    def _():
        st_re[...] = jnp.zeros_like(st_re)
        st_im[...] = jnp.zeros_like(st_im)

    rows = sre.shape[-2]
    gp = rows * LANES
    cw = uf_ref.shape[-1] // nsb
    sw = gp // nsb
    dirs = ((0, uf_ref, yf_ref), (1, ub_ref, yb_ref))
    for d, u_ref, _ in dirs:
        for b in range(batch):
            u = u_ref[b].astype(BF16)
            proj = lambda w_ref: jnp.concatenate(
                [jnp.dot(u[:, q * cw:(q + 1) * cw], w_ref[d, q], preferred_element_type=F32) for q in range(nsb)],
                axis=1).reshape(tt, rows, LANES)
            sre[d, b] = proj(bre_ref)
            sim[d, b] = proj(bim_ref)

    lam = [(lre_ref[d], lim_ref[d]) for d in range(2)]

    def body(s, carry):
        out = []
        for d in range(2):
            t = s if d == 0 else tt - 1 - s
            ar, ai = lam[d]
            for b in range(batch):
                xr, xi = carry[d * batch + b]
                nr = ar * xr - ai * xi + sre[d, b, t]
                ni = ar * xi + ai * xr + sim[d, b, t]
                sre[d, b, t] = nr
                sim[d, b, t] = ni
                out.append((nr, ni))
        return tuple(out)

    init = tuple((st_re[d, b], st_im[d, b]) for d in range(2) for b in range(batch))
    fin = lax.fori_loop(0, tt, body, init, unroll=8)
    for d in range(2):
        for b in range(batch):
            st_re[d, b], st_im[d, b] = fin[d * batch + b]

    for d, _, y_ref in dirs:
        for b in range(batch):
            xr = sre[d, b].reshape(tt, gp).astype(BF16)
            xi = sim[d, b].reshape(tt, gp).astype(BF16)
            y_ref[b] = jnp.concatenate(
                [jnp.dot(xr[:, q * sw:(q + 1) * sw], cre_ref[d, q], preferred_element_type=F32)
                 - jnp.dot(xi[:, q * sw:(q + 1) * sw], cim_ref[d, q], preferred_element_type=F32)
                 for q in range(nsb)], axis=1)


def _s5_out_kernel(yf_ref, yb_ref, u_ref, dsk_ref, gw_ref, gb_ref, o_ref):
    y = u_ref[...] * dsk_ref[...] + yf_ref[...] + yb_ref[...]
    zg = jax.nn.gelu(y)
    o_ref[...] = (zg * jax.nn.sigmoid(_dot(zg, gw_ref[...]) + gb_ref[...])).astype(BF16)


def _s5(u, batch, lam_re, lam_im, log_dt, b_re, b_im, c_re, c_im, d_skip, glu_w, glu_b):
    T, W = u.shape
    S = T // batch
    G, P, Hh = lam_re.shape[1], lam_re.shape[2], S5_GROUP
    GP = G * P
    dt = jnp.exp(log_dt)[..., None]
    mag = jnp.exp(lam_re * dt)
    ab_re, ab_im = mag * jnp.cos(lam_im * dt), mag * jnp.sin(lam_im * dt)
    den = lam_re * lam_re + lam_im * lam_im
    nr, ni = ab_re - 1.0, ab_im
    cr = (nr * lam_re + ni * lam_im) / den
    ci = (ni * lam_re - nr * lam_im) / den
    bb_re = cr[..., None] * b_re - ci[..., None] * b_im
    bb_im = cr[..., None] * b_im + ci[..., None] * b_re
    nsb = G // S5_SUPER
    eye = jnp.eye(S5_SUPER, dtype=F32)
    sb = lambda m: m.reshape(2, nsb, S5_SUPER, *m.shape[2:])
    to_in = lambda m: (eye[None, None, :, None, :, None] * sb(m).transpose(0, 1, 2, 4, 3)[:, :, :, :, None, :]
                       ).reshape(2, nsb, S5_SUPER * Hh, S5_SUPER * P).astype(BF16)
    to_out = lambda m: (eye[None, None, :, None, :, None] * sb(m).transpose(0, 1, 2, 4, 3)[:, :, :, :, None, :]
                        ).reshape(2, nsb, S5_SUPER * P, S5_SUPER * Hh).astype(BF16)
    bmat_re, bmat_im = to_in(bb_re), to_in(bb_im)
    cmat_re, cmat_im = to_out(c_re), to_out(c_im)
    rows = GP // LANES
    lam_r = ab_re.reshape(2, rows, LANES)
    lam_i = ab_im.reshape(2, rows, LANES)

    tt = min(128, S)
    nT = S // tt
    u3 = u.reshape(batch, S, W)
    fwd = pl.BlockSpec((batch, tt, W), lambda i: (0, i, 0))
    bwd = pl.BlockSpec((batch, tt, W), lambda i: (0, nT - 1 - i, 0))
    full = lambda shape: pl.BlockSpec(shape, lambda i: (0,) * len(shape))
    y_f, y_b = pl.pallas_call(
        functools.partial(_s5_scan_kernel, tt=tt, batch=batch, nsb=nsb),
        grid=(nT,),
        in_specs=[fwd, bwd, full(bmat_re.shape), full(bmat_im.shape), full(cmat_re.shape), full(cmat_im.shape),
                  full(lam_r.shape), full(lam_i.shape)],
        out_specs=[fwd, bwd],
        out_shape=[jax.ShapeDtypeStruct((batch, S, W), F32)] * 2,
        scratch_shapes=[pltpu.VMEM((2, batch, tt, rows, LANES), F32)] * 2
        + [pltpu.VMEM((2, batch, rows, LANES), F32)] * 2,
        compiler_params=_params("arbitrary"),
        name="s5_scan",
    )(u3, u3, bmat_re, bmat_im, cmat_re, cmat_im, lam_r, lam_i)

    tm = min(512, T)
    tile = pl.BlockSpec((tm, W), lambda i: (i, 0))
    return pl.pallas_call(
        _s5_out_kernel,
        grid=(T // tm,),
        in_specs=[tile, tile, tile, full((1, W)), full((W, W)), full((1, W))],
        out_specs=tile,
        out_shape=jax.ShapeDtypeStruct((T, W), BF16),
        compiler_params=_params("parallel"),
        name="s5_out",
    )(y_f.reshape(T, W), y_b.reshape(T, W), u, d_skip.reshape(1, W), glu_w.astype(BF16), glu_b.reshape(1, W))


def _outproj_ln_kernel(x_ref, a_ref, b_ref, c_ref, wa_ref, wb_ref, wc_ref, lw_ref, lb_ref, o_ref, *, alpha):
    mix = (jnp.dot(a_ref[...], wa_ref[...], preferred_element_type=F32)
           + jnp.dot(b_ref[...], wb_ref[...], preferred_element_type=F32)
           + jnp.dot(c_ref[...], wc_ref[...], preferred_element_type=F32))
    o_ref[...] = _layernorm(alpha * x_ref[...] + mix, lw_ref[...], lb_ref[...])


def _outproj_ln(x, ya, yb, yc, w_out_b, ln_w, ln_b, alpha):
    T, D = x.shape
    wa, wb, wc = ya.shape[1], yb.shape[1], yc.shape[1]
    assert wa == wb and (wa + wb) % wc == 0
    tm = min(256, T)
    rowt = lambda w: pl.BlockSpec((tm, w), lambda i: (i, 0))
    full = lambda shape: pl.BlockSpec(shape, lambda i: (0,) * len(shape))
    return pl.pallas_call(
        functools.partial(_outproj_ln_kernel, alpha=alpha),
        grid=(T // tm,),
        in_specs=[rowt(D), rowt(wa), rowt(wb), rowt(wc),
                  pl.BlockSpec((wa, D), lambda i: (0, 0)), pl.BlockSpec((wb, D), lambda i: (1, 0)),
                  pl.BlockSpec((wc, D), lambda i: ((wa + wb) // wc, 0)), full((1, D)), full((1, D))],
        out_specs=rowt(D),
        out_shape=jax.ShapeDtypeStruct((T, D), F32),
        compiler_params=_params("parallel"),
        name="outproj_ln",
    )(x, ya, yb, yc, w_out_b, w_out_b, w_out_b, ln_w.reshape(1, D), ln_b.reshape(1, D))


def _first_max(vals):
    best = vals[0]
    for v in vals[1:]:
        best = jnp.maximum(best, v)
    idx = jnp.full(best.shape, len(vals) - 1, jnp.int32)
    for j in range(len(vals) - 2, -1, -1):
        idx = jnp.where(vals[j] == best, j, idx)
    return best, idx


def _route(x, w_ref, b_ref):
    logits = _dot32(x, w_ref[...]) + b_ref[...]
    ng, ne = N_GROUPS, EXPERTS_PER_GROUP
    gl = [logits[:, j:j + 1] for j in range(ng)]
    gmax, _ = _first_max(gl)
    ex = [jnp.exp(v - gmax) for v in gl]
    den = ex[0]
    for v in ex[1:]:
        den = den + v
    prob = [v / den for v in ex]
    p_top, grp = _first_max(prob)
    el = []
    for e in range(ne):
        acc = jnp.zeros_like(p_top)
        for g in range(ng):
            acc = jnp.where(grp == g, logits[:, ng + g * ne + e:ng + g * ne + e + 1], acc)
        el.append(acc)
    v1, i1 = _first_max(el)
    el2 = [jnp.where(i1 == e, -jnp.inf, el[e]) for e in range(ne)]
    v2, i2 = _first_max(el2)
    e2 = jnp.exp(v2 - v1)
    w1 = p_top * (1.0 / (1.0 + e2))
    w2 = p_top * (e2 / (1.0 + e2))
    lane = lax.broadcasted_iota(jnp.int32, logits.shape, 1)
    return (jnp.where(lane == grp * ne + i1, w1, 0.0) + jnp.where(lane == grp * ne + i2, w2, 0.0)
            + jnp.where(lane == ng * ne, grp.astype(F32), 0.0))


def _router_kernel(x_ref, w_ref, b_ref, gate_ref):
    gate_ref[...] = _route(x_ref[...], w_ref, b_ref)


def _router(x, router_g, router_g_b, router_e, router_e_b):
    T, D = x.shape
    ng, ne = N_GROUPS, EXPERTS_PER_GROUP
    w = jnp.concatenate([router_g, router_e.transpose(1, 0, 2).reshape(D, ng * ne)], axis=1)
    w = jnp.pad(w, ((0, 0), (0, LANES - w.shape[1])))
    b = jnp.concatenate([router_g_b, router_e_b.reshape(ng * ne)])
    b = jnp.pad(b, (0, LANES - b.shape[0])).reshape(1, LANES)
    tm = min(512, T)
    return pl.pallas_call(
        _router_kernel,
        grid=(T // tm,),
        in_specs=[pl.BlockSpec((tm, D), lambda i: (i, 0)), pl.BlockSpec((D, LANES), lambda i: (0, 0)),
                  pl.BlockSpec((1, LANES), lambda i: (0, 0))],
        out_specs=pl.BlockSpec((tm, LANES), lambda i: (i, 0)),
        out_shape=jax.ShapeDtypeStruct((T, LANES), F32),
        compiler_params=_params("parallel"),
        name="router",
    )(x, w, b)


def _expert_mlp(xin, w1_ref, w3_ref, w2_ref):
    h1 = jnp.dot(xin, w1_ref[0], preferred_element_type=F32)
    h3 = jnp.dot(xin, w3_ref[0], preferred_element_type=F32)
    return _dot((h1 * jax.nn.sigmoid(h1)) * h3, w2_ref[0])


def _lane_pick(vals, lane, j):
    return jnp.sum(jnp.where(lane == j, vals, 0.0), axis=1, keepdims=True)


def _moe_ln_kernel(x_ref, gate_ref, tri_ref, w1_ref, w3_ref, w2_ref, lw_ref, lb_ref, acc,
                   xb_ref, rank, pt, xg, yg, gg, fits, *, alpha, n_experts, epg, cap):
    e = pl.program_id(1)
    g_idx = e // epg
    k = e % epg
    gate = gate_ref[...]
    lane = lax.broadcasted_iota(jnp.int32, gate.shape, 1)
    grp = _lane_pick(gate, lane, n_experts)

    nsub, sub, _ = pt.shape
    rows = lambda h: slice(h * sub, (h + 1) * sub)
    slots = lambda h: slice(h * cap, (h + 1) * cap)

    @pl.when(e == 0)
    def _():
        acc[...] = jnp.zeros_like(acc)
        xb_ref[...] = x_ref[...].astype(BF16)
        onehot = jnp.where(jnp.logical_and(lane < n_experts // epg, lane.astype(F32) == grp), 1.0, 0.0)
        most = jnp.zeros((1, LANES), F32)
        for h in range(nsub):
            rank[rows(h)] = jnp.dot(tri_ref[...], onehot[rows(h)].astype(BF16), preferred_element_type=F32)
            most = jnp.maximum(most, jnp.sum(onehot[rows(h)], axis=0, keepdims=True))
        fits[0] = jnp.where(jnp.max(most) <= cap, 1, 0)

    sparse = fits[0] == 1

    @pl.when(jnp.logical_and(sparse, k == 0))
    def _():
        slot = _lane_pick(rank[...], lane, g_idx)
        col = lax.broadcasted_iota(jnp.int32, (sub, cap), 1).astype(F32)
        g1 = gate.astype(BF16)
        r1 = gate - g1.astype(F32)
        g2 = r1.astype(BF16)
        g3 = (r1 - g2.astype(F32)).astype(BF16)
        for h in range(nsub):
            mine = grp[rows(h)] == g_idx.astype(F32)
            p = jnp.where(jnp.logical_and(mine, slot[rows(h)] == col), 1.0, 0.0).astype(BF16)
            pt[h] = p
            xg[slots(h)] = _dot_tn(p, xb_ref[rows(h)]).astype(BF16)
            gg[slots(h)] = _dot_tn(p, g1[rows(h)]) + _dot_tn(p, g2[rows(h)]) + _dot_tn(p, g3[rows(h)])
        yg[...] = jnp.zeros_like(yg)

    @pl.when(sparse)
    def _():
        lane_c = lax.broadcasted_iota(jnp.int32, gg.shape, 1)
        yg[...] += _lane_pick(gg[...], lane_c, e) * _expert_mlp(xg[...], w1_ref, w3_ref, w2_ref)

    @pl.when(jnp.logical_and(sparse, k == epg - 1))
    def _():
        for h in range(nsub):
            y = yg[slots(h)]
            y1 = y.astype(BF16)
            y2 = (y - y1.astype(F32)).astype(BF16)
            p = pt[h]
            acc[rows(h)] += (jnp.dot(p, y1, preferred_element_type=F32)
                             + jnp.dot(p, y2, preferred_element_type=F32))

    @pl.when(jnp.logical_not(sparse))
    def _():
        acc[...] += _lane_pick(gate, lane, e) * _expert_mlp(xb_ref[...], w1_ref, w3_ref, w2_ref)

    @pl.when(e == n_experts - 1)
    def _():
        acc[...] = _layernorm(alpha * x_ref[...] + acc[...], lw_ref[...], lb_ref[...])


def _moe_ln(x, gate, w1b, w3b, w2b, ln_w, ln_b, alpha):
    T, D = x.shape
    E, _, F = w1b.shape
    tm = min(MOE_TILE, T)
    sub = min(MOE_SUBTILE, tm)
    nsub = tm // sub
    cap = MOE_CAP_NUM * sub // MOE_CAP_DEN
    tri = (jnp.arange(sub)[None, :] < jnp.arange(sub)[:, None]).astype(BF16)
    once = pl.Buffered(1)
    full = lambda shape: pl.BlockSpec(shape, lambda i, e: (0,) * len(shape))
    return pl.pallas_call(
        functools.partial(_moe_ln_kernel, alpha=alpha, n_experts=E, epg=EXPERTS_PER_GROUP, cap=cap),
        grid=(T // tm, E),
        in_specs=[pl.BlockSpec((tm, D), lambda i, e: (i, 0), pipeline_mode=once),
                  pl.BlockSpec((tm, LANES), lambda i, e: (i, 0)),
                  pl.BlockSpec((sub, sub), lambda i, e: (0, 0), pipeline_mode=once),
                  pl.BlockSpec((1, D, F), lambda i, e: (e, 0, 0)), pl.BlockSpec((1, D, F), lambda i, e: (e, 0, 0)),
                  pl.BlockSpec((1, F, D), lambda i, e: (e, 0, 0)), full((1, D)), full((1, D))],
        out_specs=pl.BlockSpec((tm, D), lambda i, e: (i, 0), pipeline_mode=once),
        out_shape=jax.ShapeDtypeStruct((T, D), F32),
        scratch_shapes=[pltpu.VMEM((tm, D), BF16), pltpu.VMEM((tm, LANES), F32), pltpu.VMEM((nsub, sub, cap), BF16),
                        pltpu.VMEM((nsub * cap, D), BF16), pltpu.VMEM((nsub * cap, D), F32),
                        pltpu.VMEM((nsub * cap, LANES), F32), pltpu.SMEM((1,), jnp.int32)],
        compiler_params=_params("parallel", "arbitrary"),
        name="moe_ln",
    )(x, gate, tri, w1b, w3b, w2b, ln_w.reshape(1, D), ln_b.reshape(1, D))


def _ple_ln_kernel(x_ref, p_ref, wg_ref, wp_ref, lw_ref, lb_ref, o_ref, ob_ref, *, alpha):
    x = x_ref[...]
    gate = jax.nn.sigmoid(_dot(x, wg_ref[...]))
    ple = gate * _dot(p_ref[...], wp_ref[...])
    y = _layernorm(alpha * x + ple, lw_ref[...], lb_ref[...])
    o_ref[...] = y
    ob_ref[...] = y.astype(BF16)


def _ple_ln(x, p, wgb, wpb, ln_w, ln_b, alpha):
    T, D = x.shape
    Pd = p.shape[1]
    tm = min(256, T)
    rowt = lambda w: pl.BlockSpec((tm, w), lambda i: (i, 0))
    full = lambda shape: pl.BlockSpec(shape, lambda i: (0,) * len(shape))
    return pl.pallas_call(
        functools.partial(_ple_ln_kernel, alpha=alpha),
        grid=(T // tm,),
        in_specs=[rowt(D), rowt(Pd), full((D, D)), full((Pd, D)), full((1, D)), full((1, D))],
        out_specs=[rowt(D), rowt(D)],
        out_shape=[jax.ShapeDtypeStruct((T, D), F32), jax.ShapeDtypeStruct((T, D), BF16)],
        compiler_params=_params("parallel"),
        name="ple_ln",
    )(x, p, wgb, wpb, ln_w.reshape(1, D), ln_b.reshape(1, D))


def kernel(x, p, positions, w_in, w_out, rwkv_mu_prev, rwkv_mu_next, rwkv_w0, rwkv_w_up, rwkv_a0, rwkv_a_up, rwkv_g_up, rwkv_k_k, rwkv_k_a, rwkv_r_k, rwkv_lnx_w, rwkv_lnx_b, s5_lam_re, s5_lam_im, s5_log_dt, s5_b_re, s5_b_im, s5_c_re, s5_c_im, s5_d, s5_glu_w, s5_glu_b, moe_router_g, moe_router_g_b, moe_router_e, moe_router_e_b, moe_w1, moe_w3, moe_w2, ple_proj, ple_gate, ln_w, ln_b):
    B, S, D = x.shape
    depth = w_in.shape[0]
    T = B * S
    alpha = (2.0 * depth) ** 0.25
    ret_w = rwkv_w = rwkv_w0.shape[-1]
    s5_w = s5_d.shape[-1]
    ret_cols = 4 * ret_w
    rwkv_cols = rwkv_mu_prev.shape[-1]

    cos2, sin2 = _rope_tables(positions)
    xf = x.reshape(T, D)
    xb = xf.astype(BF16)
    for i in range(depth):
        wi = w_in[i].astype(BF16)
        z_ret = _matmul(xb, wi[:, :ret_cols], 1024, 1024)
        z_rwkv = _matmul(xb, wi[:, ret_cols:ret_cols + rwkv_cols], 1024, rwkv_cols // 3)
        z_s5 = _matmul(xb, wi[:, ret_cols + rwkv_cols:], 1024, s5_w)
        y_ret = _retention(z_ret, cos2, sin2, B)
        y_rwkv = _rwkv7(z_rwkv, B, rwkv_mu_prev[i], rwkv_mu_next[i], rwkv_w0[i], rwkv_w_up[i], rwkv_a0[i],
                        rwkv_a_up[i], rwkv_g_up[i], rwkv_k_k[i], rwkv_k_a[i], rwkv_r_k[i], rwkv_lnx_w[i],
                        rwkv_lnx_b[i])
        y_s5 = _s5(z_s5, B, s5_lam_re[i], s5_lam_im[i], s5_log_dt[i], s5_b_re[i], s5_b_im[i], s5_c_re[i],
                   s5_c_im[i], s5_d[i], s5_glu_w[i], s5_glu_b[i])
        xf = _outproj_ln(xf, y_ret, y_rwkv, y_s5, w_out[i].astype(BF16), ln_w[i, 0], ln_b[i, 0], alpha)
        gate = _router(xf, moe_router_g[i], moe_router_g_b[i], moe_router_e[i], moe_router_e_b[i])
        xf = _moe_ln(xf, gate, moe_w1[i].astype(BF16), moe_w3[i].astype(BF16), moe_w2[i].astype(BF16),
                     ln_w[i, 1], ln_b[i, 1], alpha)
        xf, xb = _ple_ln(xf, p[i].reshape(T, -1), ple_gate[i].astype(BF16), ple_proj[i].astype(BF16),
                         ln_w[i, 2], ln_b[i, 2], alpha)
    return xf.reshape(B, S, D)
```

```python
import functools
import math

import jax
import jax.numpy as jnp
import numpy as np
from jax import lax
from jax.experimental import pallas as pl
from jax.experimental.pallas import tpu as pltpu

F32 = jnp.float32
BF16 = jnp.bfloat16
HIGHEST = lax.Precision.HIGHEST

LANES = 128
VMEM_LIMIT = 56 * 1024 * 1024

RET_HD = 128
RET_CHUNK = 128
ROPE_BASE = 10000.0
RWKV_HD = 64
RWKV_CHUNK = 64
RWKV_HEADS_PER_GROUP = 4
DECAY_LORA = 64
AAA_LORA = 64
GATE_LORA = 128
S5_GROUP = 16
S5_STATE = 64
S5_SUPER = 8
N_GROUPS = 4
EXPERTS_PER_GROUP = 4
MOE_TILE = 1024
MOE_SUBTILE = 512
MOE_CAP_NUM, MOE_CAP_DEN = 5, 16
LN_EPS = 1e-5
RWKV_LNX_EPS = 64e-5


def _params(*sem):
    return pltpu.CompilerParams(dimension_semantics=sem, vmem_limit_bytes=VMEM_LIMIT)


def _dot(a, b):
    return jnp.dot(a.astype(BF16), b.astype(BF16), preferred_element_type=F32)


def _dot_nt(a, b):
    return lax.dot_general(a.astype(BF16), b.astype(BF16), (((1,), (1,)), ((), ())),
                           preferred_element_type=F32)


def _dot_tn(a, b):
    return lax.dot_general(a.astype(BF16), b.astype(BF16), (((0,), (0,)), ((), ())),
                           preferred_element_type=F32)


def _dot32(a, b):
    return jnp.dot(a, b, precision=HIGHEST, preferred_element_type=F32)


def _dot32_nt(a, b):
    return lax.dot_general(a, b, (((1,), (1,)), ((), ())), precision=HIGHEST,
                           preferred_element_type=F32)


def _dot32_tn(a, b):
    return lax.dot_general(a, b, (((0,), (0,)), ((), ())), precision=HIGHEST,
                           preferred_element_type=F32)


def _layernorm(h, w, b):
    mu = jnp.mean(h, axis=-1, keepdims=True)
    d = h - mu
    var = jnp.mean(d * d, axis=-1, keepdims=True)
    return d * lax.rsqrt(var + LN_EPS) * w + b


def _matmul_kernel(x_ref, w_ref, o_ref):
    o_ref[...] = jnp.dot(x_ref[...], w_ref[...], preferred_element_type=F32)


def _matmul(xb, wb, tm, tn):
    T, K = xb.shape
    N = wb.shape[1]
    tm = min(tm, T)
    return pl.pallas_call(
        _matmul_kernel,
        grid=(N // tn, T // tm),
        in_specs=[pl.BlockSpec((tm, K), lambda j, i: (i, 0)),
                  pl.BlockSpec((K, tn), lambda j, i: (0, j))],
        out_specs=pl.BlockSpec((tm, tn), lambda j, i: (i, j)),
        out_shape=jax.ShapeDtypeStruct((T, N), F32),
        compiler_params=_params("parallel", "parallel"),
        name="in_proj",
    )(xb, wb)


def _rope_table_kernel(pos_ref, inv_ref, sign_ref, cos_ref, sin_ref):
    ang = pos_ref[...].astype(F32) * inv_ref[...]
    cos_ref[...] = jnp.cos(ang)
    sin_ref[...] = jnp.sin(ang) * sign_ref[...]


def _rope_tables(positions):
    T = positions.size
    half = RET_HD // 2
    inv = ROPE_BASE ** (-jnp.arange(half, dtype=F32) / half)
    inv2 = jnp.concatenate([inv, inv])[None, :]
    sign = jnp.concatenate([-jnp.ones((half,), F32), jnp.ones((half,), F32)])[None, :]
    tm = min(1024, T)
    row = pl.BlockSpec((1, RET_HD), lambda i: (0, 0))
    out = pl.BlockSpec((tm, RET_HD), lambda i: (i, 0))
    return pl.pallas_call(
        _rope_table_kernel,
        grid=(T // tm,),
        in_specs=[pl.BlockSpec((tm, 1), lambda i: (i, 0)), row, row],
        out_specs=[out, out],
        out_shape=[jax.ShapeDtypeStruct((T, RET_HD), F32)] * 2,
        compiler_params=_params("parallel"),
        name="rope_tables",
    )(positions.reshape(T, 1), inv2, sign)


def _rope(t, cos2, sin2):
    return t * cos2 + pltpu.roll(t, RET_HD // 2, 1) * sin2


def _ret_bwd_kernel(k_ref, v_ref, cos_ref, sin_ref, dv_ref, dc_ref, sb_ref, state, *, heads):
    @pl.when(pl.program_id(1) == 0)
    def _():
        state[...] = jnp.zeros_like(state)

    cos2, sin2 = cos_ref[...], sin_ref[...]
    for h in range(heads):
        sl = slice(h * RET_HD, (h + 1) * RET_HD)
        k = _rope(k_ref[:, sl], cos2, sin2)
        s = state[h]
        sb_ref[0, 0, h] = s.astype(BF16)
        state[h] = dc_ref[h] * s + _dot_tn(k * dv_ref[h, 1], v_ref[:, sl])


def _ret_fwd_kernel(q_ref, k_ref, v_ref, g_ref, cos_ref, sin_ref, dm_ref, dv_ref, dc_ref, sb_ref,
                    o_ref, state, *, heads):
    @pl.when(pl.program_id(1) == 0)
    def _():
        state[...] = jnp.zeros_like(state)

    cos2, sin2 = cos_ref[...], sin_ref[...]
    for h in range(heads):
        sl = slice(h * RET_HD, (h + 1) * RET_HD)
        q = _rope(q_ref[:, sl], cos2, sin2) * (RET_HD ** -0.5)
        k = _rope(k_ref[:, sl], cos2, sin2)
        v = v_ref[:, sl]
        s = state[h]
        y = _dot(_dot_nt(q, k) * dm_ref[h], v)
        y = y + _dot(q * dv_ref[h, 2], s)
        y = y + jnp.dot((q * dv_ref[h, 3]).astype(BF16), sb_ref[0, 0, h], preferred_element_type=F32)
        state[h] = dc_ref[h] * s + _dot_tn(k * dv_ref[h, 0], v)
        y = y * lax.rsqrt(jnp.mean(y * y, axis=-1, keepdims=True) + 1e-6)
        g = g_ref[:, sl]
        o_ref[:, sl] = (y * (g * jax.nn.sigmoid(g))).astype(BF16)


def _retention(z_ret, cos2, sin2, batch):
    T, W4 = z_ret.shape
    W = W4 // 4
    H, C, Dh = W // RET_HD, RET_CHUNK, RET_HD
    nC = T // batch // C
    lg = jnp.log(1.0 - 2.0 ** (-5.0 - jnp.arange(H, dtype=F32)))
    pos = jnp.arange(C, dtype=F32)
    dmat = jnp.exp(lg[:, None, None] * jnp.abs(pos[:, None] - pos[None, :]))
    dvec = jnp.stack([jnp.exp(lg[:, None] * (C - 1 - pos)), jnp.exp(lg[:, None] * pos),
                      jnp.exp(lg[:, None] * (pos + 1.0)), jnp.exp(lg[:, None] * (C - pos))], axis=1)
    dvec = jnp.broadcast_to(dvec[..., None], (H, 4, C, Dh))
    dcay = jnp.broadcast_to(jnp.exp(lg * C)[:, None, None], (H, 1, Dh))

    def rows(f):
        return lambda b, n: (b * nC + f(n), 0)

    def cols(f, j):
        return lambda b, n: (b * nC + f(n), j)

    rev = lambda n: nC - 1 - n
    same = lambda n: n
    const3 = lambda b, n: (0, 0, 0)
    const4 = lambda b, n: (0, 0, 0, 0)
    sb = pl.pallas_call(
        functools.partial(_ret_bwd_kernel, heads=H),
        grid=(batch, nC),
        in_specs=[pl.BlockSpec((C, W), cols(rev, 1)), pl.BlockSpec((C, W), cols(rev, 2)),
                  pl.BlockSpec((C, Dh), rows(rev)), pl.BlockSpec((C, Dh), rows(rev)),
                  pl.BlockSpec((H, 4, C, Dh), const4), pl.BlockSpec((H, 1, Dh), const3)],
        out_specs=pl.BlockSpec((1, 1, H, Dh, Dh), lambda b, n: (b, nC - 1 - n, 0, 0, 0)),
        out_shape=jax.ShapeDtypeStruct((batch, nC, H, Dh, Dh), BF16),
        scratch_shapes=[pltpu.VMEM((H, Dh, Dh), F32)],
        compiler_params=_params("parallel", "arbitrary"),
        name="ret_bwd",
    )(z_ret, z_ret, cos2, sin2, dvec, dcay)
    return pl.pallas_call(
        functools.partial(_ret_fwd_kernel, heads=H),
        grid=(batch, nC),
        in_specs=[pl.BlockSpec((C, W), cols(same, 0)), pl.BlockSpec((C, W), cols(same, 1)),
                  pl.BlockSpec((C, W), cols(same, 2)), pl.BlockSpec((C, W), cols(same, 3)),
                  pl.BlockSpec((C, Dh), rows(same)), pl.BlockSpec((C, Dh), rows(same)),
                  pl.BlockSpec((H, C, C), const3), pl.BlockSpec((H, 4, C, Dh), const4),
                  pl.BlockSpec((H, 1, Dh), const3),
                  pl.BlockSpec((1, 1, H, Dh, Dh), lambda b, n: (b, n, 0, 0, 0))],
        out_specs=pl.BlockSpec((C, W), rows(same)),
        out_shape=jax.ShapeDtypeStruct((T, W), BF16),
        scratch_shapes=[pltpu.VMEM((H, Dh, Dh), F32)],
        compiler_params=_params("parallel", "arbitrary"),
        name="ret_fwd",
    )(z_ret, z_ret, z_ret, z_ret, cos2, sin2, dmat, dvec, dcay, sb)


def _head_sum(x, e_ref):
    n = e_ref.shape[0]
    parts = [_dot32(x[:, j:j + n], e_ref[...]) for j in range(0, x.shape[1], n)]
    return jnp.concatenate(parts, axis=1)


def _rwkv_pre_kernel(z_ref, zp_ref, zn_ref, mup_ref, mun_ref, w0_ref, wup_ref, a0_ref, aup_ref, gup_ref,
                     kk_w_ref, ka_ref, rk_ref, e_ref,
                     r_out, v_out, kk_out, kd_out, bd_out, lw_out, g_out, bonus_out, *, tiles_per_seq, W):
    i = pl.program_id(0)
    z = z_ref[...]
    tm = z.shape[0]
    row = lax.broadcasted_iota(jnp.int32, z.shape, 0)
    first = (i % tiles_per_seq) == 0
    last = (i % tiles_per_seq) == tiles_per_seq - 1
    prev_row = jnp.where(first, 0.0, zp_ref[7:8, :])
    next_row = jnp.where(last, 0.0, zn_ref[0:1, :])
    z_prev = jnp.where(row == 0, prev_row, pltpu.roll(z, 1, 0))
    z_next = jnp.where(row == tm - 1, next_row, pltpu.roll(z, tm - 1, 0))
    zf = z + mup_ref[...] * (z_prev - z) + mun_ref[...] * (z_next - z)
    r, k, v = zf[:, 0:W], zf[:, W:2 * W], zf[:, 2 * W:3 * W]
    o = 3 * W
    wd = jnp.tanh(zf[:, o:o + 2 * DECAY_LORA])
    ad = zf[:, o + 2 * DECAY_LORA:o + 2 * DECAY_LORA + 2 * AAA_LORA]
    gd = jax.nn.sigmoid(zf[:, o + 2 * DECAY_LORA + 2 * AAA_LORA:])
    kk = k * kk_w_ref[...]
    kk = kk / jnp.maximum(jnp.sqrt(_head_sum(kk * kk, e_ref)), 1e-12)
    r_out[...] = r
    v_out[...] = v
    kk_out[...] = kk
    g_out[...] = _dot(gd, gup_ref[...])
    bonus_out[...] = _head_sum(r * k * rk_ref[...], e_ref) * v
    for d in range(2):
        w_raw = w0_ref[d] + _dot(wd, wup_ref[d])
        lw_out[d] = (-math.exp(-0.5)) * jax.nn.sigmoid(w_raw)
        a = jax.nn.sigmoid(a0_ref[d] + _dot(ad, aup_ref[d]))
        kd_out[d] = k * (1.0 + (a - 1.0) * ka_ref[...])
        bd_out[d] = kk * a


def _split_dot(m, x):
    mb = m.astype(BF16)
    x1 = x.astype(BF16)
    r1 = x - x1.astype(F32)
    x2 = r1.astype(BF16)
    x3 = (r1 - x2.astype(F32)).astype(BF16)
    dot = lambda t: jnp.dot(mb, t, preferred_element_type=F32)
    return dot(x1) + dot(x2) + dot(x3)


def _rwkv_chunk(direction, sl, r_ref, v_ref, kk_ref, kd_ref, bd_ref, lw, cs, ctot, y_ref, state, *, L, hpg):
    gw = hpg * L
    sgn = 1 if direction == 0 else -1
    lane = lax.broadcasted_iota(jnp.int32, (L, gw), 1)
    rowc = lax.broadcasted_iota(jnp.int32, (L, gw), 0) * sgn
    colc = jnp.bitwise_and(lane, L - 1) * sgn
    strict = colc < rowc
    incl = colc <= rowc
    eye_cat = jnp.where(colc == rowc, 1.0, 0.0).astype(F32)
    shift = int(math.log2(L))
    head_of_lane = jnp.right_shift(lane, shift)
    r2 = jnp.right_shift(lax.broadcasted_iota(jnp.int32, (gw, gw), 0), shift)
    c2 = jnp.right_shift(lax.broadcasted_iota(jnp.int32, (gw, gw), 1), shift)
    same_head = r2 == c2

    def bd(x):
        return jnp.concatenate([jnp.where(head_of_lane == h, x, 0.0) for h in range(hpg)], axis=0).astype(BF16)

    c_in = cs[:, sl]
    c_ex = c_in - lw[:, sl]
    einv = jnp.exp(-c_in)
    wend = jnp.exp(ctot[:, sl] - c_in)
    b, kd, vp = bd_ref[:, sl], kd_ref[:, sl], v_ref[:, sl]
    kq = kk_ref[:, sl] * jnp.exp(c_ex)
    rq = r_ref[:, sl] * jnp.exp(c_in)
    lhs = jnp.concatenate([kq, rq], axis=0).astype(BF16)
    g = _dot_nt(lhs, jnp.concatenate([bd(b * einv), bd(kd * einv)], axis=0))
    s = state[...]
    kr = _dot_nt(lhs, s)
    yield
    a_ab = jnp.where(strict, g[:L, :gw], 0.0)
    a_ak = jnp.where(strict, g[:L, gw:], 0.0)
    a_rb = jnp.where(incl, g[L:, :gw], 0.0)
    a_rk = jnp.where(incl, g[L:, gw:], 0.0)
    bvp = bd(vp)
    rhs = kr[:L] + _dot(a_ak, bvp)
    inv = eye_cat - a_ab
    apow = a_ab
    for _ in range(shift - 1):
        apow = _dot(apow, bd(apow))
        yield
        inv = inv + _dot(inv, bd(apow))
    yield
    u = _dot(inv, bd(-rhs))
    yield
    y = kr[L:] + _dot(jnp.concatenate([a_rb, a_rk], axis=1), jnp.concatenate([bd(u), bvp], axis=0))
    y_ref[:, sl] = y
    upd = _dot_tn(jnp.concatenate([u, vp], axis=0), jnp.concatenate([b * wend, kd * wend], axis=0))
    state[...] = s * jnp.exp(ctot[:, sl]) + jnp.where(same_head, upd, 0.0)


def _rwkv_scan_kernel(rf_ref, vf_ref, kkf_ref, rb_ref, vb_ref, kkb_ref, kdf_ref, bdf_ref, lwf_ref,
                      kdb_ref, bdb_ref, lwb_ref, yf_ref, yb_ref, state, *, L, hpg, ngroups, batch):
    @pl.when(pl.program_id(0) == 0)
    def _():
        state[...] = jnp.zeros_like(state)

    row = lax.broadcasted_iota(jnp.int32, (L, L), 0)
    col = lax.broadcasted_iota(jnp.int32, (L, L), 1)
    gw = hpg * L
    chains = []
    for direction, (r_ref, v_ref, kk_ref, kd_ref, bd_ref, lw_ref, y_ref) in enumerate(
            [(rf_ref, vf_ref, kkf_ref, kdf_ref, bdf_ref, lwf_ref, yf_ref),
             (rb_ref, vb_ref, kkb_ref, kdb_ref, bdb_ref, lwb_ref, yb_ref)]):
        tri = jnp.where(col <= row if direction == 0 else col >= row, 1.0, 0.0).astype(F32)
        for b in range(batch):
            lw = lw_ref[0, b]
            cs = _split_dot(tri, lw)
            ctot = jnp.sum(lw, axis=0, keepdims=True)
            for gi in range(ngroups):
                chains.append(_rwkv_chunk(direction, slice(gi * gw, (gi + 1) * gw), r_ref.at[b], v_ref.at[b],
                                          kk_ref.at[b], kd_ref.at[0, b], bd_ref.at[0, b], lw, cs, ctot, y_ref.at[b],
                                          state.at[direction, b, gi], L=L, hpg=hpg))
    while chains:
        alive = []
        for chain in chains:
            if next(chain, "done") != "done":
                alive.append(chain)
        chains = alive


def _rwkv_post_kernel(yf_ref, yb_ref, g_ref, bonus_ref, lw_ref, lb_ref, e_ref, o_ref):
    y = yf_ref[...] + yb_ref[...]
    mu = _head_sum(y, e_ref) * (1.0 / RWKV_HD)
    dlt = y - mu
    var = _head_sum(dlt * dlt, e_ref) * (1.0 / RWKV_HD)
    yn = dlt * lax.rsqrt(var + RWKV_LNX_EPS) * lw_ref[...] + lb_ref[...]
    o_ref[...] = ((yn + bonus_ref[...]) * g_ref[...]).astype(BF16)


def _rwkv7(z, batch, mu_prev, mu_next, w0, w_up, a0, a_up, g_up, k_k, k_a, r_k, lnx_w, lnx_b):
    T, cols = z.shape
    W = w0.shape[-1]
    S = T // batch
    L = RWKV_CHUNK
    nC = S // L
    row = lambda x: x.reshape(1, -1).astype(F32)
    wup = jnp.zeros((2, 2 * DECAY_LORA, W), F32)
    wup = wup.at[0, :DECAY_LORA].set(w_up[0]).at[1, DECAY_LORA:].set(w_up[1]).astype(BF16)
    aup = jnp.zeros((2, 2 * AAA_LORA, W), F32)
    aup = aup.at[0, :AAA_LORA].set(a_up[0]).at[1, AAA_LORA:].set(a_up[1]).astype(BF16)
    eb = 4 * RWKV_HD
    hid = jnp.arange(eb) // RWKV_HD
    e = (hid[:, None] == hid[None, :]).astype(F32)
    tm = min(256, S)
    nt = T // tm
    full = lambda shape: pl.BlockSpec(shape, lambda i: (0,) * len(shape))
    tile = pl.BlockSpec((tm, W), lambda i: (i, 0))
    tile2 = pl.BlockSpec((2, tm, W), lambda i: (0, i, 0))
    hb = tm // 8
    r, v, kk, kd, bd, lw, g, bonus = pl.pallas_call(
        functools.partial(_rwkv_pre_kernel, tiles_per_seq=S // tm, W=W),
        grid=(nt,),
        in_specs=[pl.BlockSpec((tm, cols), lambda i: (i, 0)),
                  pl.BlockSpec((8, cols), lambda i: (jnp.maximum(i * hb - 1, 0), 0)),
                  pl.BlockSpec((8, cols), lambda i: (jnp.minimum((i + 1) * hb, T // 8 - 1), 0)),
                  full((1, cols)), full((1, cols)), full((2, 1, W)), full((2, 2 * DECAY_LORA, W)),
                  full((2, 1, W)), full((2, 2 * AAA_LORA, W)), full((GATE_LORA, W)),
                  full((1, W)), full((1, W)), full((1, W)), full((eb, eb))],
        out_specs=[tile, tile, tile, tile2, tile2, tile2, tile, tile],
        out_shape=[jax.ShapeDtypeStruct((T, W), F32)] * 3 + [jax.ShapeDtypeStruct((2, T, W), F32)] * 3
        + [jax.ShapeDtypeStruct((T, W), F32)] * 2,
        compiler_params=_params("parallel"),
        name="rwkv_pre",
    )(z, z, z, row(mu_prev), row(mu_next), w0.reshape(2, 1, W), wup, a0.reshape(2, 1, W), aup,
      g_up.astype(BF16), row(k_k), row(k_a), row(r_k), e)

    hpg = RWKV_HEADS_PER_GROUP
    gw = hpg * L
    fw = lambda c: c
    bw = lambda c: nC - 1 - c
    seq = lambda f: pl.BlockSpec((batch, L, W), lambda c: (0, f(c), 0))
    seq2 = lambda d, f: pl.BlockSpec((1, batch, L, W), lambda c: (d, 0, f(c), 0))
    r3, v3, kk3 = (t.reshape(batch, S, W) for t in (r, v, kk))
    kd4, bd4, lw4 = (t.reshape(2, batch, S, W) for t in (kd, bd, lw))
    y_f, y_b = pl.pallas_call(
        functools.partial(_rwkv_scan_kernel, L=L, hpg=hpg, ngroups=W // gw, batch=batch),
        grid=(nC,),
        in_specs=[seq(fw)] * 3 + [seq(bw)] * 3 + [seq2(0, fw)] * 3 + [seq2(1, bw)] * 3,
        out_specs=[seq(fw), seq(bw)],
        out_shape=[jax.ShapeDtypeStruct((batch, S, W), F32)] * 2,
        scratch_shapes=[pltpu.VMEM((2, batch, W // gw, gw, gw), F32)],
        compiler_params=_params("arbitrary"),
        name="rwkv_scan",
    )(r3, v3, kk3, r3, v3, kk3, kd4, bd4, lw4, kd4, bd4, lw4)
    y_f = y_f.reshape(T, W)
    y_b = y_b.reshape(T, W)

    return pl.pallas_call(
        _rwkv_post_kernel,
        grid=(nt,),
        in_specs=[tile, tile, tile, tile, full((1, W)), full((1, W)), full((eb, eb))],
        out_specs=tile,
        out_shape=jax.ShapeDtypeStruct((T, W), BF16),
        compiler_params=_params("parallel"),
        name="rwkv_post",
    )(y_f, y_b, g, bonus, row(lnx_w), row(lnx_b), e)


def _s5_scan_kernel(uf_ref, ub_ref, bre_ref, bim_ref, cre_ref, cim_ref, lre_ref, lim_ref, yf_ref, yb_ref,
                    sre, sim, st_re, st_im, *, tt, batch, nsb):
    @pl.when(pl.program_id(0) == 0)
    def _():
        st_re[...] = jnp.zeros_like(st_re)
        st_im[...] = jnp.zeros_like(st_im)

    rows = sre.shape[-2]
    gp = rows * LANES
    cw = uf_ref.shape[-1] // nsb
    sw = gp // nsb
    dirs = ((0, uf_ref, yf_ref), (1, ub_ref, yb_ref))
    for d, u_ref, _ in dirs:
        for b in range(batch):
            u = u_ref[b].astype(BF16)
            proj = lambda w_ref: jnp.concatenate(
                [jnp.dot(u[:, q * cw:(q + 1) * cw], w_ref[d, q], preferred_element_type=F32) for q in range(nsb)],
                axis=1).reshape(tt, rows, LANES)
            sre[d, b] = proj(bre_ref)
            sim[d, b] = proj(bim_ref)

    lam = [(lre_ref[d], lim_ref[d]) for d in range(2)]

    def body(s, carry):
        out = []
        for d in range(2):
            t = s if d == 0 else tt - 1 - s
            ar, ai = lam[d]
            for b in range(batch):
                xr, xi = carry[d * batch + b]
                nr = ar * xr - ai * xi + sre[d, b, t]
                ni = ar * xi + ai * xr + sim[d, b, t]
                sre[d, b, t] = nr
                sim[d, b, t] = ni
                out.append((nr, ni))
        return tuple(out)

    init = tuple((st_re[d, b], st_im[d, b]) for d in range(2) for b in range(batch))
    fin = lax.fori_loop(0, tt, body, init, unroll=8)
    for d in range(2):
        for b in range(batch):
            st_re[d, b], st_im[d, b] = fin[d * batch + b]

    for d, _, y_ref in dirs:
        for b in range(batch):
            xr = sre[d, b].reshape(tt, gp).astype(BF16)
            xi = sim[d, b].reshape(tt, gp).astype(BF16)
            y_ref[b] = jnp.concatenate(
                [jnp.dot(xr[:, q * sw:(q + 1) * sw], cre_ref[d, q], preferred_element_type=F32)
                 - jnp.dot(xi[:, q * sw:(q + 1) * sw], cim_ref[d, q], preferred_element_type=F32)
                 for q in range(nsb)], axis=1)


def _s5_out_kernel(yf_ref, yb_ref, u_ref, dsk_ref, gw_ref, gb_ref, o_ref):
    y = u_ref[...] * dsk_ref[...] + yf_ref[...] + yb_ref[...]
    zg = jax.nn.gelu(y)
    o_ref[...] = (zg * jax.nn.sigmoid(_dot(zg, gw_ref[...]) + gb_ref[...])).astype(BF16)


def _s5(u, batch, lam_re, lam_im, log_dt, b_re, b_im, c_re, c_im, d_skip, glu_w, glu_b):
    T, W = u.shape
    S = T // batch
    G, P, Hh = lam_re.shape[1], lam_re.shape[2], S5_GROUP
    GP = G * P
    dt = jnp.exp(log_dt)[..., None]
    mag = jnp.exp(lam_re * dt)
    ab_re, ab_im = mag * jnp.cos(lam_im * dt), mag * jnp.sin(lam_im * dt)
    den = lam_re * lam_re + lam_im * lam_im
    nr, ni = ab_re - 1.0, ab_im
    cr = (nr * lam_re + ni * lam_im) / den
    ci = (ni * lam_re - nr * lam_im) / den
    bb_re = cr[..., None] * b_re - ci[..., None] * b_im
    bb_im = cr[..., None] * b_im + ci[..., None] * b_re
    nsb = G // S5_SUPER
    eye = jnp.eye(S5_SUPER, dtype=F32)
    sb = lambda m: m.reshape(2, nsb, S5_SUPER, *m.shape[2:])
    to_in = lambda m: (eye[None, None, :, None, :, None] * sb(m).transpose(0, 1, 2, 4, 3)[:, :, :, :, None, :]
                       ).reshape(2, nsb, S5_SUPER * Hh, S5_SUPER * P).astype(BF16)
    to_out = lambda m: (eye[None, None, :, None, :, None] * sb(m).transpose(0, 1, 2, 4, 3)[:, :, :, :, None, :]
                        ).reshape(2, nsb, S5_SUPER * P, S5_SUPER * Hh).astype(BF16)
    bmat_re, bmat_im = to_in(bb_re), to_in(bb_im)
    cmat_re, cmat_im = to_out(c_re), to_out(c_im)
    rows = GP // LANES
    lam_r = ab_re.reshape(2, rows, LANES)
    lam_i = ab_im.reshape(2, rows, LANES)

    tt = min(128, S)
    nT = S // tt
    u3 = u.reshape(batch, S, W)
    fwd = pl.BlockSpec((batch, tt, W), lambda i: (0, i, 0))
    bwd = pl.BlockSpec((batch, tt, W), lambda i: (0, nT - 1 - i, 0))
    full = lambda shape: pl.BlockSpec(shape, lambda i: (0,) * len(shape))
    y_f, y_b = pl.pallas_call(
        functools.partial(_s5_scan_kernel, tt=tt, batch=batch, nsb=nsb),
        grid=(nT,),
        in_specs=[fwd, bwd, full(bmat_re.shape), full(bmat_im.shape), full(cmat_re.shape), full(cmat_im.shape),
                  full(lam_r.shape), full(lam_i.shape)],
        out_specs=[fwd, bwd],
        out_shape=[jax.ShapeDtypeStruct((batch, S, W), F32)] * 2,
        scratch_shapes=[pltpu.VMEM((2, batch, tt, rows, LANES), F32)] * 2
        + [pltpu.VMEM((2, batch, rows, LANES), F32)] * 2,
        compiler_params=_params("arbitrary"),
        name="s5_scan",
    )(u3, u3, bmat_re, bmat_im, cmat_re, cmat_im, lam_r, lam_i)

    tm = min(512, T)
    tile = pl.BlockSpec((tm, W), lambda i: (i, 0))
    return pl.pallas_call(
        _s5_out_kernel,
        grid=(T // tm,),
        in_specs=[tile, tile, tile, full((1, W)), full((W, W)), full((1, W))],
        out_specs=tile,
        out_shape=jax.ShapeDtypeStruct((T, W), BF16),
        compiler_params=_params("parallel"),
        name="s5_out",
    )(y_f.reshape(T, W), y_b.reshape(T, W), u, d_skip.reshape(1, W), glu_w.astype(BF16), glu_b.reshape(1, W))


def _outproj_ln_kernel(x_ref, a_ref, b_ref, c_ref, wa_ref, wb_ref, wc_ref, lw_ref, lb_ref, o_ref, *, alpha):
    mix = (jnp.dot(a_ref[...], wa_ref[...], preferred_element_type=F32)
           + jnp.dot(b_ref[...], wb_ref[...], preferred_element_type=F32)
           + jnp.dot(c_ref[...], wc_ref[...], preferred_element_type=F32))
    o_ref[...] = _layernorm(alpha * x_ref[...] + mix, lw_ref[...], lb_ref[...])


def _outproj_ln(x, ya, yb, yc, w_out_b, ln_w, ln_b, alpha):
    T, D = x.shape
    wa, wb, wc = ya.shape[1], yb.shape[1], yc.shape[1]
    assert wa == wb and (wa + wb) % wc == 0
    tm = min(256, T)
    rowt = lambda w: pl.BlockSpec((tm, w), lambda i: (i, 0))
    full = lambda shape: pl.BlockSpec(shape, lambda i: (0,) * len(shape))
    return pl.pallas_call(
        functools.partial(_outproj_ln_kernel, alpha=alpha),
        grid=(T // tm,),
        in_specs=[rowt(D), rowt(wa), rowt(wb), rowt(wc),
                  pl.BlockSpec((wa, D), lambda i: (0, 0)), pl.BlockSpec((wb, D), lambda i: (1, 0)),
                  pl.BlockSpec((wc, D), lambda i: ((wa + wb) // wc, 0)), full((1, D)), full((1, D))],
        out_specs=rowt(D),
        out_shape=jax.ShapeDtypeStruct((T, D), F32),
        compiler_params=_params("parallel"),
        name="outproj_ln",
    )(x, ya, yb, yc, w_out_b, w_out_b, w_out_b, ln_w.reshape(1, D), ln_b.reshape(1, D))


def _first_max(vals):
    best = vals[0]
    for v in vals[1:]:
        best = jnp.maximum(best, v)
    idx = jnp.full(best.shape, len(vals) - 1, jnp.int32)
    for j in range(len(vals) - 2, -1, -1):
        idx = jnp.where(vals[j] == best, j, idx)
    return best, idx


def _route(x, w_ref, b_ref):
    logits = _dot32(x, w_ref[...]) + b_ref[...]
    ng, ne = N_GROUPS, EXPERTS_PER_GROUP
    gl = [logits[:, j:j + 1] for j in range(ng)]
    gmax, _ = _first_max(gl)
    ex = [jnp.exp(v - gmax) for v in gl]
    den = ex[0]
    for v in ex[1:]:
        den = den + v
    prob = [v / den for v in ex]
    p_top, grp = _first_max(prob)
    el = []
    for e in range(ne):
        acc = jnp.zeros_like(p_top)
        for g in range(ng):
            acc = jnp.where(grp == g, logits[:, ng + g * ne + e:ng + g * ne + e + 1], acc)
        el.append(acc)
    v1, i1 = _first_max(el)
    el2 = [jnp.where(i1 == e, -jnp.inf, el[e]) for e in range(ne)]
    v2, i2 = _first_max(el2)
    e2 = jnp.exp(v2 - v1)
    w1 = p_top * (1.0 / (1.0 + e2))
    w2 = p_top * (e2 / (1.0 + e2))
    lane = lax.broadcasted_iota(jnp.int32, logits.shape, 1)
    return (jnp.where(lane == grp * ne + i1, w1, 0.0) + jnp.where(lane == grp * ne + i2, w2, 0.0)
            + jnp.where(lane == ng * ne, grp.astype(F32), 0.0))


def _router_kernel(x_ref, w_ref, b_ref, gate_ref):
    gate_ref[...] = _route(x_ref[...], w_ref, b_ref)


def _router(x, router_g, router_g_b, router_e, router_e_b):
    T, D = x.shape
    ng, ne = N_GROUPS, EXPERTS_PER_GROUP
    w = jnp.concatenate([router_g, router_e.transpose(1, 0, 2).reshape(D, ng * ne)], axis=1)
    w = jnp.pad(w, ((0, 0), (0, LANES - w.shape[1])))
    b = jnp.concatenate([router_g_b, router_e_b.reshape(ng * ne)])
    b = jnp.pad(b, (0, LANES - b.shape[0])).reshape(1, LANES)
    tm = min(512, T)
    return pl.pallas_call(
        _router_kernel,
        grid=(T // tm,),
        in_specs=[pl.BlockSpec((tm, D), lambda i: (i, 0)), pl.BlockSpec((D, LANES), lambda i: (0, 0)),
                  pl.BlockSpec((1, LANES), lambda i: (0, 0))],
        out_specs=pl.BlockSpec((tm, LANES), lambda i: (i, 0)),
        out_shape=jax.ShapeDtypeStruct((T, LANES), F32),
        compiler_params=_params("parallel"),
        name="router",
    )(x, w, b)


def _expert_mlp(xin, w1_ref, w3_ref, w2_ref):
    h1 = jnp.dot(xin, w1_ref[0], preferred_element_type=F32)
    h3 = jnp.dot(xin, w3_ref[0], preferred_element_type=F32)
    return _dot((h1 * jax.nn.sigmoid(h1)) * h3, w2_ref[0])


def _lane_pick(vals, lane, j):
    return jnp.sum(jnp.where(lane == j, vals, 0.0), axis=1, keepdims=True)


def _moe_ln_kernel(x_ref, gate_ref, tri_ref, w1_ref, w3_ref, w2_ref, lw_ref, lb_ref, acc,
                   xb_ref, rank, pt, xg, yg, gg, fits, *, alpha, n_experts, epg, cap):
    e = pl.program_id(1)
    g_idx = e // epg
    k = e % epg
    gate = gate_ref[...]
    lane = lax.broadcasted_iota(jnp.int32, gate.shape, 1)
    grp = _lane_pick(gate, lane, n_experts)

    nsub, sub, _ = pt.shape
    rows = lambda h: slice(h * sub, (h + 1) * sub)
    slots = lambda h: slice(h * cap, (h + 1) * cap)

    @pl.when(e == 0)
    def _():
        acc[...] = jnp.zeros_like(acc)
        xb_ref[...] = x_ref[...].astype(BF16)
        onehot = jnp.where(jnp.logical_and(lane < n_experts // epg, lane.astype(F32) == grp), 1.0, 0.0)
        most = jnp.zeros((1, LANES), F32)
        for h in range(nsub):
            rank[rows(h)] = jnp.dot(tri_ref[...], onehot[rows(h)].astype(BF16), preferred_element_type=F32)
            most = jnp.maximum(most, jnp.sum(onehot[rows(h)], axis=0, keepdims=True))
        fits[0] = jnp.where(jnp.max(most) <= cap, 1, 0)

    sparse = fits[0] == 1

    @pl.when(jnp.logical_and(sparse, k == 0))
    def _():
        slot = _lane_pick(rank[...], lane, g_idx)
        col = lax.broadcasted_iota(jnp.int32, (sub, cap), 1).astype(F32)
        g1 = gate.astype(BF16)
        r1 = gate - g1.astype(F32)
        g2 = r1.astype(BF16)
        g3 = (r1 - g2.astype(F32)).astype(BF16)
        for h in range(nsub):
            mine = grp[rows(h)] == g_idx.astype(F32)
            p = jnp.where(jnp.logical_and(mine, slot[rows(h)] == col), 1.0, 0.0).astype(BF16)
            pt[h] = p
            xg[slots(h)] = _dot_tn(p, xb_ref[rows(h)]).astype(BF16)
            gg[slots(h)] = _dot_tn(p, g1[rows(h)]) + _dot_tn(p, g2[rows(h)]) + _dot_tn(p, g3[rows(h)])
        yg[...] = jnp.zeros_like(yg)

    @pl.when(sparse)
    def _():
        lane_c = lax.broadcasted_iota(jnp.int32, gg.shape, 1)
        yg[...] += _lane_pick(gg[...], lane_c, e) * _expert_mlp(xg[...], w1_ref, w3_ref, w2_ref)

    @pl.when(jnp.logical_and(sparse, k == epg - 1))
    def _():
        for h in range(nsub):
            y = yg[slots(h)]
            y1 = y.astype(BF16)
            y2 = (y - y1.astype(F32)).astype(BF16)
            p = pt[h]
            acc[rows(h)] += (jnp.dot(p, y1, preferred_element_type=F32)
                             + jnp.dot(p, y2, preferred_element_type=F32))

    @pl.when(jnp.logical_not(sparse))
    def _():
        acc[...] += _lane_pick(gate, lane, e) * _expert_mlp(xb_ref[...], w1_ref, w3_ref, w2_ref)

    @pl.when(e == n_experts - 1)
    def _():
        acc[...] = _layernorm(alpha * x_ref[...] + acc[...], lw_ref[...], lb_ref[...])


def _moe_ln(x, gate, w1b, w3b, w2b, ln_w, ln_b, alpha):
    T, D = x.shape
    E, _, F = w1b.shape
    tm = min(MOE_TILE, T)
    sub = min(MOE_SUBTILE, tm)
    nsub = tm // sub
    cap = MOE_CAP_NUM * sub // MOE_CAP_DEN
    tri = (jnp.arange(sub)[None, :] < jnp.arange(sub)[:, None]).astype(BF16)
    once = pl.Buffered(1)
    full = lambda shape: pl.BlockSpec(shape, lambda i, e: (0,) * len(shape))
    return pl.pallas_call(
        functools.partial(_moe_ln_kernel, alpha=alpha, n_experts=E, epg=EXPERTS_PER_GROUP, cap=cap),
        grid=(T // tm, E),
        in_specs=[pl.BlockSpec((tm, D), lambda i, e: (i, 0), pipeline_mode=once),
                  pl.BlockSpec((tm, LANES), lambda i, e: (i, 0)),
                  pl.BlockSpec((sub, sub), lambda i, e: (0, 0), pipeline_mode=once),
                  pl.BlockSpec((1, D, F), lambda i, e: (e, 0, 0)), pl.BlockSpec((1, D, F), lambda i, e: (e, 0, 0)),
                  pl.BlockSpec((1, F, D), lambda i, e: (e, 0, 0)), full((1, D)), full((1, D))],
        out_specs=pl.BlockSpec((tm, D), lambda i, e: (i, 0), pipeline_mode=once),
        out_shape=jax.ShapeDtypeStruct((T, D), F32),
        scratch_shapes=[pltpu.VMEM((tm, D), BF16), pltpu.VMEM((tm, LANES), F32), pltpu.VMEM((nsub, sub, cap), BF16),
                        pltpu.VMEM((nsub * cap, D), BF16), pltpu.VMEM((nsub * cap, D), F32),
                        pltpu.VMEM((nsub * cap, LANES), F32), pltpu.SMEM((1,), jnp.int32)],
        compiler_params=_params("parallel", "arbitrary"),
        name="moe_ln",
    )(x, gate, tri, w1b, w3b, w2b, ln_w.reshape(1, D), ln_b.reshape(1, D))


def _ple_ln_kernel(x_ref, p_ref, wg_ref, wp_ref, lw_ref, lb_ref, o_ref, ob_ref, *, alpha):
    x = x_ref[...]
    gate = jax.nn.sigmoid(_dot(x, wg_ref[...]))
    ple = gate * _dot(p_ref[...], wp_ref[...])
    y = _layernorm(alpha * x + ple, lw_ref[...], lb_ref[...])
    o_ref[...] = y
    ob_ref[...] = y.astype(BF16)


def _ple_ln(x, p, wgb, wpb, ln_w, ln_b, alpha):
    T, D = x.shape
    Pd = p.shape[1]
    tm = min(256, T)
    rowt = lambda w: pl.BlockSpec((tm, w), lambda i: (i, 0))
    full = lambda shape: pl.BlockSpec(shape, lambda i: (0,) * len(shape))
    return pl.pallas_call(
        functools.partial(_ple_ln_kernel, alpha=alpha),
        grid=(T // tm,),
        in_specs=[rowt(D), rowt(Pd), full((D, D)), full((Pd, D)), full((1, D)), full((1, D))],
        out_specs=[rowt(D), rowt(D)],
        out_shape=[jax.ShapeDtypeStruct((T, D), F32), jax.ShapeDtypeStruct((T, D), BF16)],
        compiler_params=_params("parallel"),
        name="ple_ln",
    )(x, p, wgb, wpb, ln_w.reshape(1, D), ln_b.reshape(1, D))


def kernel(x, p, positions, w_in, w_out, rwkv_mu_prev, rwkv_mu_next, rwkv_w0, rwkv_w_up, rwkv_a0, rwkv_a_up, rwkv_g_up, rwkv_k_k, rwkv_k_a, rwkv_r_k, rwkv_lnx_w, rwkv_lnx_b, s5_lam_re, s5_lam_im, s5_log_dt, s5_b_re, s5_b_im, s5_c_re, s5_c_im, s5_d, s5_glu_w, s5_glu_b, moe_router_g, moe_router_g_b, moe_router_e, moe_router_e_b, moe_w1, moe_w3, moe_w2, ple_proj, ple_gate, ln_w, ln_b):
    B, S, D = x.shape
    depth = w_in.shape[0]
    T = B * S
    alpha = (2.0 * depth) ** 0.25
    ret_w = rwkv_w = rwkv_w0.shape[-1]
    s5_w = s5_d.shape[-1]
    ret_cols = 4 * ret_w
    rwkv_cols = rwkv_mu_prev.shape[-1]

    cos2, sin2 = _rope_tables(positions)
    xf = x.reshape(T, D)
    xb = xf.astype(BF16)
    for i in range(depth):
        wi = w_in[i].astype(BF16)
        z_ret = _matmul(xb, wi[:, :ret_cols], 1024, 1024)
        z_rwkv = _matmul(xb, wi[:, ret_cols:ret_cols + rwkv_cols], 1024, rwkv_cols // 3)
        z_s5 = _matmul(xb, wi[:, ret_cols + rwkv_cols:], 1024, s5_w)
        y_ret = _retention(z_ret, cos2, sin2, B)
        y_rwkv = _rwkv7(z_rwkv, B, rwkv_mu_prev[i], rwkv_mu_next[i], rwkv_w0[i], rwkv_w_up[i], rwkv_a0[i],
                        rwkv_a_up[i], rwkv_g_up[i], rwkv_k_k[i], rwkv_k_a[i], rwkv_r_k[i], rwkv_lnx_w[i],
                        rwkv_lnx_b[i])
        y_s5 = _s5(z_s5, B, s5_lam_re[i], s5_lam_im[i], s5_log_dt[i], s5_b_re[i], s5_b_im[i], s5_c_re[i],
                   s5_c_im[i], s5_d[i], s5_glu_w[i], s5_glu_b[i])
        xf = _outproj_ln(xf, y_ret, y_rwkv, y_s5, w_out[i].astype(BF16), ln_w[i, 0], ln_b[i, 0], alpha)
        gate = _router(xf, moe_router_g[i], moe_router_g_b[i], moe_router_e[i], moe_router_e_b[i])
        xf = _moe_ln(xf, gate, moe_w1[i].astype(BF16), moe_w3[i].astype(BF16), moe_w2[i].astype(BF16),
                     ln_w[i, 1], ln_b[i, 1], alpha)
        xf, xb = _ple_ln(xf, p[i].reshape(T, -1), ple_gate[i].astype(BF16), ple_proj[i].astype(BF16),
                         ln_w[i, 2], ln_b[i, 2], alpha)
    return xf.reshape(B, S, D)
```
